```python
import jax, jax.numpy as jnp
from jax import lax
import numpy as np

D_MODEL = 1024
BATCH = 4
SEQ = 4096
DEPTH = 2

N_META = 16
BLK = 128
PAD = BLK - N_META
ROPE_THETA = 10000.0
EPS = 1e-6
NEG = -1e30

FOX_HEADS = 8
FOX_DH = 64

MLA_HEADS = 8
MLA_NOPE = 64
MLA_ROPE = 32
MLA_V = 64
MLA_QLORA = 384
MLA_KVLORA = 256

SWA_HEADS = 8
SWA_KV_HEADS = 2
SWA_DH = 64
WINDOW = 128

BRANCH_W = 512
N_BRANCH = 3

SPLIT_SIZES = (
    FOX_HEADS * FOX_DH, FOX_HEADS * FOX_DH, FOX_HEADS * FOX_DH, FOX_HEADS, BRANCH_W,
    MLA_QLORA, MLA_KVLORA, MLA_ROPE, BRANCH_W,
    SWA_HEADS * SWA_DH, SWA_KV_HEADS * SWA_DH, SWA_KV_HEADS * SWA_DH, BRANCH_W,
    N_BRANCH * D_MODEL,
)
N_IN = sum(SPLIT_SIZES)

kernel_name = "hybrid_fox_mla_swa_gated_branches"


def rmsnorm(x, g):
    xf = x.astype(jnp.float32)
    y = xf * lax.rsqrt(jnp.mean(xf * xf, axis=-1, keepdims=True) + EPS)
    return (y * g.astype(jnp.float32)).astype(x.dtype)


def rope(x, pos):
    half = x.shape[-1] // 2
    inv = ROPE_THETA ** (-jnp.arange(half, dtype=jnp.float32) / half)
    ang = pos.astype(jnp.float32)[:, None] * inv[None, :]
    cos = jnp.cos(ang)[None, :, None, :]
    sin = jnp.sin(ang)[None, :, None, :]
    xf = x.astype(jnp.float32)
    x1, x2 = xf[..., :half], xf[..., half:]
    return jnp.concatenate([x1 * cos - x2 * sin, x2 * cos + x1 * sin], axis=-1).astype(x.dtype)


def causal_block_attention(q, k, v, scale, log_cum=None):
    B, L, H, _ = q.shape
    nb = L // BLK
    kpos = jnp.arange(L)
    key_ok = kpos >= PAD
    qb = jnp.moveaxis(q.reshape(B, nb, BLK, H, q.shape[-1]), 1, 0)
    if log_cum is None:
        xs = (jnp.arange(nb), qb)
        ck = None
    else:
        cb = jnp.moveaxis(log_cum.reshape(B, nb, BLK, H), 1, 0)
        xs = (jnp.arange(nb), qb, cb)
        ck = jnp.swapaxes(log_cum, 1, 2)

    def block(args):
        i, qi = args[0], args[1]
        s = jnp.einsum('bqhd,bkhd->bhqk', qi, k).astype(jnp.float32) * scale
        if log_cum is not None:
            ci = jnp.swapaxes(args[2], 1, 2)
            s = s + (ci[..., :, None] - ck[..., None, :])
        qpos = i * BLK + jnp.arange(BLK)
        mask = (kpos[None, :] <= qpos[:, None]) & key_ok[None, :]
        s = jnp.where(mask, s, NEG)
        p = jax.nn.softmax(s, axis=-1)
        return jnp.einsum('bhqk,bkhd->bqhd', p.astype(v.dtype), v)

    o = lax.map(block, xs)
    return jnp.moveaxis(o, 0, 1).reshape(B, L, H, v.shape[-1])


def sliding_window_sink_attention(q, k, v, sinks):
    B, L, H, D = q.shape
    Hkv = k.shape[2]
    G = H // Hkv
    nb = L // BLK
    qb = q.reshape(B, nb, BLK, Hkv, G, D)

    def with_prev(t):
        prev = jnp.concatenate([jnp.zeros_like(t[:, :1]), t[:, :-1]], axis=1)
        return jnp.concatenate([prev, t], axis=2)

    kx = with_prev(k.reshape(B, nb, BLK, Hkv, D))
    vx = with_prev(v.reshape(B, nb, BLK, Hkv, D))
    s = jnp.einsum('bnqhgd,bnkhd->bnhgqk', qb, kx).astype(jnp.float32) * (D ** -0.5)
    blocks = jnp.arange(nb)[:, None]
    qpos = blocks * BLK + jnp.arange(BLK)[None, :]
    kpos = (blocks - 1) * BLK + jnp.arange(2 * BLK)[None, :]
    rel = qpos[:, :, None] - kpos[:, None, :]
    mask = (rel >= 0) & (rel < WINDOW) & (kpos >= PAD)[:, None, :]
    s = jnp.where(mask[None, :, None, None], s, NEG)
    sink = jnp.broadcast_to(sinks.astype(jnp.float32).reshape(1, 1, Hkv, G, 1, 1), s.shape[:-1] + (1,))
    p = jax.nn.softmax(jnp.concatenate([s, sink], axis=-1), axis=-1)[..., :-1]
    o = jnp.einsum('bnhgqk,bnkhd->bnqhgd', p.astype(v.dtype), vx)
    return o.reshape(B, L, H, D)


def hybrid_layer(x, pos, norm_g, w_in, b_f, g_cq, g_ckv, w_uq, w_ukv, sinks, w_branch, w_out):
    B, L, _ = x.shape
    h = rmsnorm(x, norm_g)
    proj = h @ w_in
    (a_q, a_k, a_v, a_f, a_z,
     b_cq, b_ckv, b_kr, b_z,
     c_q, c_k, c_v, c_z, gates) = jnp.split(proj, np.cumsum(SPLIT_SIZES)[:-1], axis=-1)

    log_f = jax.nn.log_sigmoid((a_f + b_f).astype(jnp.float32))
    log_cum = jnp.cumsum(log_f, axis=1)
    y_a = causal_block_attention(a_q.reshape(B, L, FOX_HEADS, FOX_DH),
                                 a_k.reshape(B, L, FOX_HEADS, FOX_DH),
                                 a_v.reshape(B, L, FOX_HEADS, FOX_DH),
                                 FOX_DH ** -0.5, log_cum)

    cq = rmsnorm(b_cq, g_cq)
    ckv = rmsnorm(b_ckv, g_ckv)
    qB = (cq @ w_uq).reshape(B, L, MLA_HEADS, MLA_NOPE + MLA_ROPE)
    q_b = jnp.concatenate([qB[..., :MLA_NOPE], rope(qB[..., MLA_NOPE:], pos)], axis=-1)
    kvB = (ckv @ w_ukv).reshape(B, L, MLA_HEADS, MLA_NOPE + MLA_V)
    k_rope = rope(b_kr.reshape(B, L, 1, MLA_ROPE), pos)
    k_b = jnp.concatenate([kvB[..., :MLA_NOPE],
                           jnp.broadcast_to(k_rope, (B, L, MLA_HEADS, MLA_ROPE))], axis=-1)
    v_b = kvB[..., MLA_NOPE:]
    y_b = causal_block_attention(q_b, k_b, v_b, (MLA_NOPE + MLA_ROPE) ** -0.5)

    qc = rope(c_q.reshape(B, L, SWA_HEADS, SWA_DH), pos)
    kc = rope(c_k.reshape(B, L, SWA_KV_HEADS, SWA_DH), pos)
    vc = c_v.reshape(B, L, SWA_KV_HEADS, SWA_DH)
    y_c = sliding_window_sink_attention(qc, kc, vc, sinks)

    branches = jnp.stack([y_a.reshape(B, L, BRANCH_W) * jax.nn.silu(a_z),
                          y_b.reshape(B, L, BRANCH_W) * jax.nn.silu(b_z),
                          y_c.reshape(B, L, BRANCH_W) * jax.nn.silu(c_z)], axis=2)
    proj_br = jnp.einsum('blnw,nwd->blnd', branches, w_branch)
    g = jax.nn.sigmoid(gates.reshape(B, L, N_BRANCH, D_MODEL))
    merged = jnp.sum(g * proj_br, axis=2)
    return x + merged @ w_out


def setup_inputs(seed: int = 0) -> dict:
    key = jax.random.key(seed)
    ks = jax.random.split(key, 14)
    f32 = jnp.float32
    nrm = lambda k, shape, scale: jax.random.normal(k, shape, f32) * scale
    return {
        "x": nrm(ks[0], (BATCH, SEQ, D_MODEL), 1.0),
        "meta_tokens": nrm(ks[1], (N_META, D_MODEL), 1.0),
        "norm_g": 1.0 + nrm(ks[2], (DEPTH, D_MODEL), 0.02),
        "w_in": nrm(ks[3], (DEPTH, D_MODEL, N_IN), D_MODEL ** -0.5),
        "b_f": jax.random.uniform(ks[4], (DEPTH, FOX_HEADS), f32, 1.0, 4.0),
        "g_cq": 1.0 + nrm(ks[5], (DEPTH, MLA_QLORA), 0.02),
        "g_ckv": 1.0 + nrm(ks[6], (DEPTH, MLA_KVLORA), 0.02),
        "w_uq": nrm(ks[7], (DEPTH, MLA_QLORA, MLA_HEADS * (MLA_NOPE + MLA_ROPE)), MLA_QLORA ** -0.5),
        "w_ukv": nrm(ks[8], (DEPTH, MLA_KVLORA, MLA_HEADS * (MLA_NOPE + MLA_V)), MLA_KVLORA ** -0.5),
        "sinks": nrm(ks[9], (DEPTH, SWA_HEADS), 0.5),
        "w_branch": nrm(ks[10], (DEPTH, N_BRANCH, BRANCH_W, D_MODEL), BRANCH_W ** -0.5),
        "w_out": nrm(ks[11], (DEPTH, D_MODEL, D_MODEL), D_MODEL ** -0.5),
        "final_g": 1.0 + nrm(ks[12], (D_MODEL,), 0.02),
    }


def reference(x, meta_tokens, norm_g, w_in, b_f, g_cq, g_ckv, w_uq, w_ukv, sinks, w_branch, w_out, final_g):
    B = x.shape[0]
    pad = jnp.zeros((B, PAD, D_MODEL), x.dtype)
    meta = jnp.broadcast_to(meta_tokens.astype(x.dtype)[None], (B, N_META, D_MODEL))
    h = jnp.concatenate([pad, meta, x], axis=1)
    pos = jnp.arange(h.shape[1]) - PAD
    for l in range(DEPTH):
        h = hybrid_layer(h, pos, norm_g[l], w_in[l], b_f[l], g_cq[l], g_ckv[l],
                         w_uq[l], w_ukv[l], sinks[l], w_branch[l], w_out[l])
    h = rmsnorm(h, final_g)
    return h[:, BLK:]
```

```python
import functools

import numpy as np
import jax
import jax.numpy as jnp
from jax import lax
from jax.experimental import pallas as pl
from jax.experimental.pallas import tpu as pltpu

F32 = jnp.float32
BF16 = jnp.bfloat16

D_MODEL = 1024
N_META = 16
BLK = 128
PAD = BLK - N_META
ROPE_THETA = 10000.0
EPS = 1e-6
NEG = -1e30

HEADS = 8
HEAD_DIM = 64
MLA_NOPE = 64
MLA_ROPE = 32
MLA_V = 64
MLA_QLORA = 384
MLA_KVLORA = 256
SWA_KV_HEADS = 2
BRANCH_W = 512
N_BRANCH = 3

LANES = 128
N_PROJ = 8192
VMEM_LIMIT = 56 * 1024 * 1024

T_GATES = 0
T_AQ = 24
T_AK = 28
T_AV = 32
T_AZ = 36
T_BZ = 40
T_CQ = 44
T_CZ = 48
T_CK = 52
T_MLA = 54
T_CV = 60
T_AF = 62

TM_IN = 768
TN_IN = 1024
TM_PREP = 384
TM_OUT = 384
TQ = 384
FOX_EXT = 6
MLA_EXT = MLA_ROPE


def _cparams(n_axes):
    return pltpu.CompilerParams(dimension_semantics=("arbitrary",) * n_axes,
                                vmem_limit_bytes=VMEM_LIMIT)


def _inproj_kernel(x_ref, g_ref, w_ref, p_ref, af_ref, h_scr):
    j = pl.program_id(1)

    @pl.when(j == 0)
    def _():
        x = x_ref[...]
        ms = jnp.mean(x * x, axis=-1, keepdims=True)
        h_scr[...] = (x * lax.rsqrt(ms + EPS) * g_ref[...]).astype(BF16)

    acc = jnp.dot(h_scr[...], w_ref[...], preferred_element_type=F32)
    p_ref[...] = acc.astype(BF16)

    @pl.when(j == (T_AF * LANES) // TN_IN)
    def _():
        off = (T_AF * LANES) % TN_IN
        af_ref[...] = acc[:, off:off + LANES]


def _inproj(x2d, g, w):
    m = x2d.shape[0]
    return pl.pallas_call(
        _inproj_kernel,
        grid=(m // TM_IN, N_PROJ // TN_IN),
        in_specs=[pl.BlockSpec((TM_IN, D_MODEL), lambda i, j: (i, 0)),
                  pl.BlockSpec((1, D_MODEL), lambda i, j: (0, 0)),
                  pl.BlockSpec((D_MODEL, TN_IN), lambda i, j: (0, j))],
        out_specs=[pl.BlockSpec((TM_IN, TN_IN), lambda i, j: (i, j)),
                   pl.BlockSpec((TM_IN, LANES), lambda i, j: (i, 0))],
        out_shape=[jax.ShapeDtypeStruct((m, N_PROJ), BF16),
                   jax.ShapeDtypeStruct((m, LANES), F32)],
        scratch_shapes=[pltpu.VMEM((TM_IN, D_MODEL), BF16)],
        compiler_params=_cparams(2),
        name="inproj",
    )(x2d, g, w)


def _rope(x, cos, sin_signed, half):
    width = x.shape[1]
    reps = width // LANES
    if reps > 1:
        cos = jnp.concatenate([cos] * reps, axis=1)
        sin_signed = jnp.concatenate([sin_signed] * reps, axis=1)
    lane = lax.broadcasted_iota(jnp.int32, x.shape, 1)
    up = pltpu.roll(x, width - half, axis=1)
    down = pltpu.roll(x, half, axis=1)
    swapped = jnp.where((lane & (2 * half - 1)) < half, up, down)
    return x * cos + swapped * sin_signed


def _mla_prep_kernel(p_ref, gq_ref, gkv_ref, wqn_ref, wqr_ref, wk_ref, wv_ref, cos_ref, sin_ref,
                     qn_ref, qr_ref, kn_ref, kr_ref, v_ref):
    blk = p_ref[...].astype(F32)
    cq = blk[:, :MLA_QLORA]
    ckv = blk[:, MLA_QLORA:MLA_QLORA + MLA_KVLORA]
    kr = blk[:, MLA_QLORA + MLA_KVLORA:]
    cq = (cq * lax.rsqrt(jnp.mean(cq * cq, axis=-1, keepdims=True) + EPS) * gq_ref[...]).astype(BF16)
    ckv = (ckv * lax.rsqrt(jnp.mean(ckv * ckv, axis=-1, keepdims=True) + EPS) * gkv_ref[...]).astype(BF16)
    scale = (MLA_NOPE + MLA_ROPE) ** -0.5
    cos = cos_ref[...]
    sin = sin_ref[...]
    qn_ref[...] = (jnp.dot(cq, wqn_ref[...], preferred_element_type=F32) * scale).astype(BF16)
    qr = jnp.dot(cq, wqr_ref[...], preferred_element_type=F32)
    qr_ref[...] = (_rope(qr, cos, sin, MLA_ROPE // 2) * scale).astype(BF16)
    kn_ref[...] = jnp.dot(ckv, wk_ref[...], preferred_element_type=F32).astype(BF16)
    v_ref[...] = jnp.dot(ckv, wv_ref[...], preferred_element_type=F32).astype(BF16)
    kr_ref[...] = _rope(kr, cos, sin, MLA_ROPE // 2).astype(BF16)


def _mla_prep(proj, gq, gkv, wqn, wqr, wk, wv, cos, sin, seq_tiles):
    m = proj.shape[0]
    full = lambda a: pl.BlockSpec(a.shape, lambda i: (0,) * a.ndim)
    row = lambda w: pl.BlockSpec((TM_PREP, w), lambda i: (i, 0))
    tab = pl.BlockSpec((TM_PREP, LANES), lambda i: (i % seq_tiles, 0))
    mla_w = MLA_QLORA + MLA_KVLORA + LANES
    return pl.pallas_call(
        _mla_prep_kernel,
        grid=(m // TM_PREP,),
        in_specs=[pl.BlockSpec((TM_PREP, mla_w), lambda i: (i, (T_MLA * LANES) // mla_w)),
                  full(gq), full(gkv), full(wqn), full(wqr), full(wk), full(wv), tab, tab],
        out_specs=[row(BRANCH_W), row(BRANCH_W), row(BRANCH_W), row(LANES), row(BRANCH_W)],
        out_shape=[jax.ShapeDtypeStruct((m, BRANCH_W), BF16),
                   jax.ShapeDtypeStruct((m, BRANCH_W), BF16),
                   jax.ShapeDtypeStruct((m, BRANCH_W), BF16),
                   jax.ShapeDtypeStruct((m, LANES), BF16),
                   jax.ShapeDtypeStruct((m, BRANCH_W), BF16)],
        compiler_params=_cparams(1),
        name="mla_prep",
    )(proj, gq, gkv, wqn, wqr, wk, wv, cos, sin)


def _split3(x):
    hi = x.astype(BF16)
    r1 = x - hi.astype(F32)
    mid = r1.astype(BF16)
    lo = (r1 - mid.astype(F32)).astype(BF16)
    return jnp.concatenate([hi, mid, lo], axis=1)


def _fox_prep_kernel(af_ref, bf_ref, selq_ref, selk_ref, oneq_ref, onek_ref, qx_ref, kx_ref,
                     lf_scr, c_scr):
    seq = af_ref.shape[0]
    x = af_ref[...] + bf_ref[...]
    lf_scr[...] = -(jnp.maximum(-x, 0.0) + jnp.log1p(jnp.exp(-jnp.abs(x))))
    r = lax.broadcasted_iota(jnp.int32, (BLK, BLK), 0)
    c = lax.broadcasted_iota(jnp.int32, (BLK, BLK), 1)
    tri = (c <= r).astype(BF16)

    def body(t, carry):
        off = pl.multiple_of(t * BLK, BLK)
        cs = jnp.dot(tri, _split3(lf_scr[pl.ds(off, BLK), :]), preferred_element_type=F32)
        cum = cs[:, :LANES] + cs[:, LANES:2 * LANES] + cs[:, 2 * LANES:] + carry
        c_scr[pl.ds(off, BLK), :] = cum
        return cum[BLK - 1:BLK, :]

    lax.fori_loop(0, seq // BLK, body, jnp.zeros((1, LANES), F32))
    parts = _split3(c_scr[...])
    qx_ref[...] = (jnp.dot(parts, selq_ref[...], preferred_element_type=F32) + oneq_ref[...]).astype(BF16)
    kx_ref[...] = (jnp.dot(parts, selk_ref[...], preferred_element_type=F32) + onek_ref[...]).astype(BF16)


def _fox_select_constants():
    selq = np.zeros((3 * LANES, 4 * LANES), np.float32)
    selk = np.zeros((3 * LANES, 4 * LANES), np.float32)
    oneq = np.zeros((1, 4 * LANES), np.float32)
    onek = np.zeros((1, 4 * LANES), np.float32)
    for p in range(HEADS // 2):
        for s in range(2):
            h = 2 * p + s
            base = p * LANES + s * FOX_EXT
            for part in range(3):
                selq[part * LANES + h, base + part] = 1.0
                selk[part * LANES + h, base + 3 + part] = -1.0
                oneq[0, base + 3 + part] = 1.0
                onek[0, base + part] = 1.0
    return (jnp.asarray(selq, BF16), jnp.asarray(selk, BF16), jnp.asarray(oneq), jnp.asarray(onek))


def _fox_prep(af, bf_row, batch, seq):
    selq, selk, oneq, onek = _fox_select_constants()
    full = lambda a: pl.BlockSpec(a.shape, lambda b: (0,) * a.ndim)
    return pl.pallas_call(
        _fox_prep_kernel,
        grid=(batch,),
        in_specs=[pl.BlockSpec((seq, LANES), lambda b: (b, 0)),
                  full(bf_row), full(selq), full(selk), full(oneq), full(onek)],
        out_specs=[pl.BlockSpec((seq, 4 * LANES), lambda b: (b, 0)),
                   pl.BlockSpec((seq, 4 * LANES), lambda b: (b, 0))],
        out_shape=[jax.ShapeDtypeStruct((batch * seq, 4 * LANES), BF16),
                   jax.ShapeDtypeStruct((batch * seq, 4 * LANES), BF16)],
        scratch_shapes=[pltpu.VMEM((seq, LANES), F32), pltpu.VMEM((seq, LANES), F32)],
        compiler_params=_cparams(1),
        name="fox_prep",
    )(af, bf_row, selq, selk, oneq, onek)


def _flash_kernel(qm_ref, qx_ref, km_ref, kx_ref, v_ref, o_ref, m_scr, l_scr, acc_scr, *, scale, ext):
    i = pl.program_id(2)
    half = LANES // 2
    lane = lax.broadcasted_iota(jnp.int32, (TQ, LANES), 1)
    qm = qm_ref[...].astype(F32) * scale
    qx = qx_ref[...].astype(F32)
    qa = jnp.concatenate([jnp.where(lane < half, qm, 0.0), jnp.where(lane < ext, qx, 0.0)], axis=1)
    qb = jnp.concatenate([jnp.where(lane >= half, qm, 0.0),
                          jnp.where((lane >= ext) & (lane < 2 * ext), qx, 0.0)], axis=1)
    q = jnp.concatenate([qa, qb], axis=0).astype(BF16)

    m_scr[...] = jnp.full(m_scr.shape, NEG, F32)
    l_scr[...] = jnp.zeros(l_scr.shape, F32)
    acc_scr[...] = jnp.zeros(acc_scr.shape, F32)

    def step(c, masked):
        off = pl.multiple_of(c * TQ, TQ)
        k = jnp.concatenate([km_ref[pl.ds(off, TQ), :], kx_ref[pl.ds(off, TQ), :]], axis=1)
        s = lax.dot_general(q, k, (((1,), (1,)), ((), ())), preferred_element_type=F32)
        if masked:
            qpos = i * TQ + lax.broadcasted_iota(jnp.int32, (TQ, TQ), 0)
            kpos = c * TQ + lax.broadcasted_iota(jnp.int32, (TQ, TQ), 1)
            ok = (kpos >= PAD) & (kpos <= qpos)
            ok = jnp.concatenate([ok, ok], axis=0)
            s = jnp.where(ok, s, NEG)
        m_old = m_scr[...]
        m_new = jnp.maximum(m_old, jnp.max(s, axis=1, keepdims=True))
        alpha = jnp.exp(m_old - m_new)
        p = jnp.exp(s - m_new)
        l_scr[...] = alpha * l_scr[...] + jnp.sum(p, axis=1, keepdims=True)
        pv = jnp.dot(p.astype(BF16), v_ref[pl.ds(off, TQ), :], preferred_element_type=F32)
        acc_scr[...] = alpha * acc_scr[...] + pv
        m_scr[...] = m_new

    step(0, True)

    def body(c, carry):
        step(c, False)
        return carry

    lax.fori_loop(1, i, body, 0)

    @pl.when(i > 0)
    def _():
        step(i, True)

    out = acc_scr[...] / l_scr[...]
    o_ref[...] = jnp.where(lane < half, out[:TQ], out[TQ:]).astype(o_ref.dtype)


def _flash(qm, qm_tile, qx, qx_tile, km, km_tile, kx, kx_tile, kx_per_pair, v, v_tile,
           batch, seq, scale, ext):
    nq = seq // TQ
    pairs = HEADS // 2
    kx_col = (lambda p: kx_tile + p) if kx_per_pair else (lambda p: kx_tile)
    return pl.pallas_call(
        functools.partial(_flash_kernel, scale=scale, ext=ext),
        grid=(batch, pairs, nq),
        in_specs=[pl.BlockSpec((TQ, LANES), lambda b, p, i: (b * nq + i, qm_tile + p)),
                  pl.BlockSpec((TQ, LANES), lambda b, p, i: (b * nq + i, qx_tile + p)),
                  pl.BlockSpec((seq, LANES), lambda b, p, i: (b, km_tile + p)),
                  pl.BlockSpec((seq, LANES), lambda b, p, i: (b, kx_col(p))),
                  pl.BlockSpec((seq, LANES), lambda b, p, i: (b, v_tile + p))],
        out_specs=pl.BlockSpec((TQ, LANES), lambda b, p, i: (b * nq + i, p)),
        out_shape=jax.ShapeDtypeStruct((batch * seq, BRANCH_W), BF16),
        scratch_shapes=[pltpu.VMEM((2 * TQ, 1), F32), pltpu.VMEM((2 * TQ, 1), F32),
                        pltpu.VMEM((2 * TQ, LANES), F32)],
        compiler_params=_cparams(3),
        name="flash",
    )(qm, qx, km, kx, v)


def _swa_kernel(sink_ref, q_ref, kp_ref, kc_ref, vp_ref, vc_ref, cosc_ref, sinc_ref, cosp_ref, sinp_ref,
                o_ref):
    n = pl.program_id(1)
    half = LANES // 2
    rh = HEAD_DIM // 2
    lane = lax.broadcasted_iota(jnp.int32, (BLK, LANES), 1)
    q = _rope(q_ref[...].astype(F32), cosc_ref[...], sinc_ref[...], rh) * HEAD_DIM ** -0.5
    kc = _rope(kc_ref[...].astype(F32), cosc_ref[...], sinc_ref[...], rh).astype(BF16)
    kp = _rope(kp_ref[...].astype(F32), cosp_ref[...], sinp_ref[...], rh).astype(BF16)
    k = jnp.concatenate([kp, kc], axis=0)
    v = jnp.concatenate([vp_ref[...], vc_ref[...]], axis=0)

    r = lax.broadcasted_iota(jnp.int32, (BLK, 2 * BLK), 0)
    c = lax.broadcasted_iota(jnp.int32, (BLK, 2 * BLK), 1)
    d = c - r
    ok = (d >= 1) & (d <= BLK) & (c >= PAD - (n - 1) * BLK)
    ok = jnp.concatenate([ok, ok], axis=0)

    tiles = []
    for t in range(HEADS // 2):
        g = t // (HEADS // 2 // SWA_KV_HEADS)
        qt = q[:, t * LANES:(t + 1) * LANES]
        qq = jnp.concatenate([jnp.where(lane < half, qt, 0.0), jnp.where(lane >= half, qt, 0.0)],
                             axis=0).astype(BF16)
        kt = k[:, g * LANES:(g + 1) * LANES]
        s = lax.dot_general(qq, kt, (((1,), (1,)), ((), ())), preferred_element_type=F32)
        s = jnp.where(ok, s, NEG)
        sink = jnp.concatenate([jnp.full((BLK, 1), sink_ref[2 * t], F32),
                                jnp.full((BLK, 1), sink_ref[2 * t + 1], F32)], axis=0)
        m = jnp.maximum(jnp.max(s, axis=1, keepdims=True), sink)
        p = jnp.exp(s - m)
        denom = jnp.sum(p, axis=1, keepdims=True) + jnp.exp(sink - m)
        o = jnp.dot(p.astype(BF16), v[:, g * LANES:(g + 1) * LANES], preferred_element_type=F32) / denom
        tiles.append(jnp.where(lane < half, o[:BLK], o[BLK:]))
    o_ref[...] = jnp.concatenate(tiles, axis=1).astype(o_ref.dtype)


def _swa(proj, sinks, cos, sin, batch, seq):
    nb = seq // BLK
    kw = 2 * LANES
    prev = lambda n: jnp.maximum(n - 1, 0)
    return pl.pallas_call(
        _swa_kernel,
        grid=(batch, nb),
        in_specs=[pl.BlockSpec(memory_space=pltpu.SMEM),
                  pl.BlockSpec((BLK, BRANCH_W), lambda b, n: (b * nb + n, (T_CQ * LANES) // BRANCH_W)),
                  pl.BlockSpec((BLK, kw), lambda b, n: (b * nb + prev(n), (T_CK * LANES) // kw)),
                  pl.BlockSpec((BLK, kw), lambda b, n: (b * nb + n, (T_CK * LANES) // kw)),
                  pl.BlockSpec((BLK, kw), lambda b, n: (b * nb + prev(n), (T_CV * LANES) // kw)),
                  pl.BlockSpec((BLK, kw), lambda b, n: (b * nb + n, (T_CV * LANES) // kw)),
                  pl.BlockSpec((BLK, LANES), lambda b, n: (n, 0)),
                  pl.BlockSpec((BLK, LANES), lambda b, n: (n, 0)),
                  pl.BlockSpec((BLK, LANES), lambda b, n: (prev(n), 0)),
                  pl.BlockSpec((BLK, LANES), lambda b, n: (prev(n), 0))],
        out_specs=pl.BlockSpec((BLK, BRANCH_W), lambda b, n: (b * nb + n, 0)),
        out_shape=jax.ShapeDtypeStruct((batch * seq, BRANCH_W), BF16),
        compiler_params=_cparams(2),
        name="swa",
    )(sinks, proj, proj, proj, proj, proj, cos, sin, cos, sin)


def _sigmoid(x):
    return 1.0 / (1.0 + jnp.exp(-x))


def _out_kernel(ya_ref, yb_ref, yc_ref, za_ref, zb_ref, zc_ref, g_ref, x_ref, wb_ref, wo_ref, fg_ref,
                o_ref, *, final_norm):
    merged = None
    for n, (y_ref, z_ref) in enumerate(((ya_ref, za_ref), (yb_ref, zb_ref), (yc_ref, zc_ref))):
        z = z_ref[...].astype(F32)
        br = (y_ref[...].astype(F32) * (z * _sigmoid(z))).astype(BF16)
        pr = jnp.dot(br, wb_ref[n], preferred_element_type=F32)
        gate = _sigmoid(g_ref[:, n * D_MODEL:(n + 1) * D_MODEL].astype(F32))
        merged = gate * pr if merged is None else merged + gate * pr
    h = x_ref[...] + jnp.dot(merged.astype(BF16), wo_ref[...], preferred_element_type=F32)
    if final_norm:
        h = h * lax.rsqrt(jnp.mean(h * h, axis=-1, keepdims=True) + EPS) * fg_ref[...]
    o_ref[...] = h


def _out(ya, yb, yc, proj, x2d, wb, wo, fg, final_norm):
    m = x2d.shape[0]
    row = lambda w, col: pl.BlockSpec((TM_OUT, w), lambda i: (i, col))
    full = lambda a: pl.BlockSpec(a.shape, lambda i: (0,) * a.ndim)
    zcol = lambda t: (t * LANES) // BRANCH_W
    return pl.pallas_call(
        functools.partial(_out_kernel, final_norm=final_norm),
        grid=(m // TM_OUT,),
        in_specs=[row(BRANCH_W, 0), row(BRANCH_W, 0), row(BRANCH_W, 0),
                  row(BRANCH_W, zcol(T_AZ)), row(BRANCH_W, zcol(T_BZ)), row(BRANCH_W, zcol(T_CZ)),
                  row(N_BRANCH * D_MODEL, 0), row(D_MODEL, 0), full(wb), full(wo), full(fg)],
        out_specs=row(D_MODEL, 0),
        out_shape=jax.ShapeDtypeStruct((m, D_MODEL), F32),
        compiler_params=_cparams(1),
        name="out",
    )(ya, yb, yc, proj, proj, proj, proj, x2d, wb, wo, fg)


def _relayout_w_in(w):
    sizes = (512, 512, 512, 8, 512, MLA_QLORA, MLA_KVLORA, MLA_ROPE, 512, 512, 128, 128, 512, 3072)
    offs = np.concatenate([[0], np.cumsum(sizes)])
    (a_q, a_k, a_v, a_f, a_z, b_cq, b_ckv, b_kr, b_z, c_q, c_k, c_v, c_z, gates) = [
        w[:, offs[n]:offs[n + 1]] for n in range(len(sizes))]
    k0, k1 = c_k[:, :HEAD_DIM], c_k[:, HEAD_DIM:]
    v0, v1 = c_v[:, :HEAD_DIM], c_v[:, HEAD_DIM:]
    zeros = lambda n: jnp.zeros((w.shape[0], n), w.dtype)
    cols = [gates, a_q, a_k, a_v, a_z, b_z, c_q, c_z,
            k0, k0, k1, k1,
            b_cq, b_ckv, b_kr, b_kr, b_kr, b_kr,
            v0, v0, v1, v1,
            a_f, zeros(LANES - 8), zeros(LANES)]
    out = jnp.concatenate(cols, axis=1).astype(BF16)
    assert out.shape[1] == N_PROJ
    return out


def _relayout_mla_weights(w_uq, w_ukv):
    uq = w_uq.reshape(MLA_QLORA, HEADS, MLA_NOPE + MLA_ROPE)
    wqn = uq[:, :, :MLA_NOPE].reshape(MLA_QLORA, HEADS * MLA_NOPE)
    rope = uq[:, :, MLA_NOPE:].reshape(MLA_QLORA, HEADS // 2, 2 * MLA_ROPE)
    wqr = jnp.concatenate([rope, jnp.zeros_like(rope)], axis=-1).reshape(MLA_QLORA, HEADS // 2 * LANES)
    ukv = w_ukv.reshape(MLA_KVLORA, HEADS, MLA_NOPE + MLA_V)
    wk = ukv[:, :, :MLA_NOPE].reshape(MLA_KVLORA, HEADS * MLA_NOPE)
    wv = ukv[:, :, MLA_NOPE:].reshape(MLA_KVLORA, HEADS * MLA_V)
    return wqn.astype(BF16), wqr.astype(BF16), wk.astype(BF16), wv.astype(BF16)


def _rope_tables(pos, half):
    inv = ROPE_THETA ** (-jnp.arange(half, dtype=F32) / half)
    ang = pos.astype(F32)[:, None] * inv[None, :]
    cos = jnp.cos(ang)
    sin = jnp.sin(ang)
    reps = LANES // (2 * half)
    cos_full = jnp.tile(jnp.concatenate([cos, cos], axis=1), (1, reps))
    sin_signed = jnp.tile(jnp.concatenate([-sin, sin], axis=1), (1, reps))
    return cos_full, sin_signed


def kernel(x, meta_tokens, norm_g, w_in, b_f, g_cq, g_ckv, w_uq, w_ukv, sinks, w_branch, w_out, final_g):
    batch, seq_in, d = x.shape
    seq = PAD + N_META + seq_in
    depth = w_in.shape[0]
    assert d == D_MODEL and seq % TQ == 0 and seq % TM_PREP == 0
    assert (batch * seq) % TM_IN == 0 and (batch * seq) % TM_OUT == 0

    pad = jnp.zeros((batch, PAD, d), x.dtype)
    meta = jnp.broadcast_to(meta_tokens.astype(x.dtype)[None], (batch, N_META, d))
    h = jnp.concatenate([pad, meta, x], axis=1).reshape(batch * seq, d)
    pos = jnp.arange(seq) - PAD
    cos16, sin16 = _rope_tables(pos, MLA_ROPE // 2)
    cos32, sin32 = _rope_tables(pos, HEAD_DIM // 2)
    fg = final_g.reshape(1, d).astype(F32)

    for l in range(depth):
        w = _relayout_w_in(w_in[l])
        wqn, wqr, wk, wv = _relayout_mla_weights(w_uq[l], w_ukv[l])
        bf_row = jnp.concatenate([b_f[l].astype(F32), jnp.zeros((LANES - HEADS,), F32)]).reshape(1, LANES)

        proj, af = _inproj(h, norm_g[l].reshape(1, d).astype(F32), w)

        qx, kx = _fox_prep(af, bf_row, batch, seq)
        y_a = _flash(proj, T_AQ, qx, 0, proj, T_AK, kx, 0, True, proj, T_AV,
                     batch, seq, HEAD_DIM ** -0.5, FOX_EXT)

        qn, qr, kn, kr, vb = _mla_prep(proj, g_cq[l].reshape(1, -1).astype(F32),
                                       g_ckv[l].reshape(1, -1).astype(F32),
                                       wqn, wqr, wk, wv, cos16, sin16, seq // TM_PREP)
        y_b = _flash(qn, 0, qr, 0, kn, 0, kr, 0, False, vb, 0, batch, seq, 1.0, MLA_EXT)

        y_c = _swa(proj, sinks[l].astype(F32), cos32, sin32, batch, seq)

        h = _out(y_a, y_b, y_c, proj, h, w_branch[l].astype(BF16), w_out[l].astype(BF16), fg,
                 final_norm=(l == depth - 1))

    return h.reshape(batch, seq, d)[:, BLK:]
```

```python
import functools

import numpy as np
import jax
import jax.numpy as jnp
from jax import lax
from jax.experimental import pallas as pl
from jax.experimental.pallas import tpu as pltpu

F32 = jnp.float32
BF16 = jnp.bfloat16

D_MODEL = 1024
N_META = 16
BLK = 128
PAD = BLK - N_META
ROPE_THETA = 10000.0
EPS = 1e-6
NEG = -1e30

HEADS = 8
HEAD_DIM = 64
MLA_NOPE = 64
MLA_ROPE = 32
MLA_V = 64
MLA_QLORA = 384
MLA_KVLORA = 256
SWA_KV_HEADS = 2
BRANCH_W = 512
N_BRANCH = 3

LANES = 128
N_PROJ = 8192
VMEM_LIMIT = 56 * 1024 * 1024

T_GATES = 0
T_AQ = 24
T_AK = 28
T_AV = 32
T_AZ = 36
T_BZ = 40
T_CQ = 44
T_CZ = 48
T_CK = 52
T_MLA = 54
T_CV = 60
T_AF = 62

TM_IN = 1408
TN_IN = 2048
TM_PREP = 1408
TM_OUT = 384
TM_FINAL = 512
TQ = 384
CW = 256
NS = 4
SWA_QB = 3
LOG2E = 1.4426950408889634
FOX_EXT = 6
MLA_EXT = MLA_ROPE


def _cparams(n_axes):
    return pltpu.CompilerParams(dimension_semantics=("arbitrary",) * n_axes,
                                vmem_limit_bytes=VMEM_LIMIT)


def _first_layer_rows(x_ref, head_ref, tile_in_seq):
    xb = x_ref[...]
    shifted = jnp.concatenate([head_ref[...], xb[:xb.shape[0] - BLK]], axis=0)
    return jnp.where(tile_in_seq == 0, shifted, xb)


def _first_layer_spec(tm, seq, seq_in, tile_of):
    tiles = seq // tm

    def index_map(*idx):
        t = tile_of(*idx)
        start = (t // tiles) * seq_in + jnp.maximum((t % tiles) * tm - BLK, 0)
        return pl.multiple_of(start, BLK), 0

    return pl.BlockSpec((pl.Element(tm), pl.Element(D_MODEL)), index_map)


def _inproj_kernel(x_ref, head_ref, g_ref, w_ref, p_ref, af_ref, h_scr, *, first, tiles_per_seq):
    j = pl.program_id(1)

    @pl.when(j == 0)
    def _():
        if first:
            x = _first_layer_rows(x_ref, head_ref, pl.program_id(0) % tiles_per_seq)
        else:
            x = x_ref[...]
        ms = jnp.mean(x * x, axis=-1, keepdims=True)
        h_scr[...] = (x * lax.rsqrt(ms + EPS) * g_ref[...]).astype(BF16)

    acc = jnp.dot(h_scr[...], w_ref[...], preferred_element_type=F32)
    p_ref[...] = acc.astype(BF16)

    @pl.when(j == (T_AF * LANES) // TN_IN)
    def _():
        off = (T_AF * LANES) % TN_IN
        af_ref[...] = acc[:, off:off + LANES]


def _layer_block(a, l):
    return pl.BlockSpec((None,) + a.shape[1:], lambda *_: (l,) + (0,) * (a.ndim - 1))


def _inproj(x2d, head, g, w, l, batch, seq, first):
    m = batch * seq
    if first:
        x_spec = _first_layer_spec(TM_IN, seq, x2d.shape[0] // batch, lambda i, j: i)
    else:
        x_spec = pl.BlockSpec((TM_IN, D_MODEL), lambda i, j: (i, 0))
    return pl.pallas_call(
        functools.partial(_inproj_kernel, first=first, tiles_per_seq=seq // TM_IN),
        grid=(m // TM_IN, N_PROJ // TN_IN),
        in_specs=[x_spec,
                  pl.BlockSpec(head.shape, lambda i, j: (0, 0)),
                  pl.BlockSpec((1, D_MODEL), lambda i, j: (0, 0)),
                  pl.BlockSpec((None, D_MODEL, TN_IN), lambda i, j: (l, 0, j))],
        out_specs=[pl.BlockSpec((TM_IN, TN_IN), lambda i, j: (i, j)),
                   pl.BlockSpec((TM_IN, LANES), lambda i, j: (i, 0))],
        out_shape=[jax.ShapeDtypeStruct((m, N_PROJ), BF16),
                   jax.ShapeDtypeStruct((m, LANES), F32)],
        scratch_shapes=[pltpu.VMEM((TM_IN, D_MODEL), BF16)],
        compiler_params=_cparams(2),
        name="inproj",
    )(x2d, head, g, w)


def _rope(x, cos, sin_signed, half):
    width = x.shape[1]
    reps = width // LANES
    if reps > 1:
        cos = jnp.concatenate([cos] * reps, axis=1)
        sin_signed = jnp.concatenate([sin_signed] * reps, axis=1)
    lane = lax.broadcasted_iota(jnp.int32, x.shape, 1)
    up = pltpu.roll(x, width - half, axis=1)
    down = pltpu.roll(x, half, axis=1)
    swapped = jnp.where((lane & (2 * half - 1)) < half, up, down)
    return x * cos + swapped * sin_signed


def _mla_prep_kernel(p_ref, gq_ref, gkv_ref, wqn_ref, wqr_ref, wk_ref, wv_ref, cos_ref, sin_ref,
                     qn_ref, qr_ref, kn_ref, kr_ref, v_ref, *, seq_tiles):
    blk = p_ref[...].astype(F32)
    cq = blk[:, :MLA_QLORA]
    ckv = blk[:, MLA_QLORA:MLA_QLORA + MLA_KVLORA]
    kr = blk[:, MLA_QLORA + MLA_KVLORA:]
    cq = (cq * lax.rsqrt(jnp.mean(cq * cq, axis=-1, keepdims=True) + EPS) * gq_ref[...]).astype(BF16)
    ckv = (ckv * lax.rsqrt(jnp.mean(ckv * ckv, axis=-1, keepdims=True) + EPS) * gkv_ref[...]).astype(BF16)
    scale = (MLA_NOPE + MLA_ROPE) ** -0.5
    cos = cos_ref[...]
    sin = sin_ref[...]
    qn_ref[...] = jnp.dot(cq, wqn_ref[...], preferred_element_type=F32).astype(BF16)
    qr = jnp.dot(cq, wqr_ref[...], preferred_element_type=F32)
    qr = _rope(qr, cos, sin, MLA_ROPE // 2) * (scale * LOG2E)
    qlane = lax.broadcasted_iota(jnp.int32, qr.shape, 1) & (LANES - 1)
    qr_ref[...] = jnp.where(qlane == 2 * MLA_EXT, 1.0, qr).astype(BF16)
    kn_ref[...] = jnp.dot(ckv, wk_ref[...], preferred_element_type=F32).astype(BF16)
    v_ref[...] = jnp.dot(ckv, wv_ref[...], preferred_element_type=F32).astype(BF16)
    kr = _rope(kr, cos, sin, MLA_ROPE // 2)
    pos_in_seq = (pl.program_id(0) % seq_tiles) * kr.shape[0] + lax.broadcasted_iota(jnp.int32, kr.shape, 0)
    klane = lax.broadcasted_iota(jnp.int32, kr.shape, 1)
    kr_ref[...] = jnp.where(klane == 2 * MLA_EXT, jnp.where(pos_in_seq < PAD, NEG, 0.0), kr).astype(BF16)


def _mla_prep(proj, gq, gkv, wqn, wqr, wk, wv, l, cos, sin, seq_tiles):
    m = proj.shape[0]
    full = lambda a: pl.BlockSpec(a.shape, lambda i: (0,) * a.ndim)
    layer = lambda a: _layer_block(a, l)
    row = lambda w: pl.BlockSpec((TM_PREP, w), lambda i: (i, 0))
    tab = pl.BlockSpec((TM_PREP, LANES), lambda i: (i % seq_tiles, 0))
    mla_w = MLA_QLORA + MLA_KVLORA + LANES
    return pl.pallas_call(
        functools.partial(_mla_prep_kernel, seq_tiles=seq_tiles),
        grid=(m // TM_PREP,),
        in_specs=[pl.BlockSpec((TM_PREP, mla_w), lambda i: (i, (T_MLA * LANES) // mla_w)),
                  full(gq), full(gkv), layer(wqn), layer(wqr), layer(wk), layer(wv), tab, tab],
        out_specs=[row(BRANCH_W), row(BRANCH_W), row(BRANCH_W), row(LANES), row(BRANCH_W)],
        out_shape=[jax.ShapeDtypeStruct((m, BRANCH_W), BF16),
                   jax.ShapeDtypeStruct((m, BRANCH_W), BF16),
                   jax.ShapeDtypeStruct((m, BRANCH_W), BF16),
                   jax.ShapeDtypeStruct((m, LANES), BF16),
                   jax.ShapeDtypeStruct((m, BRANCH_W), BF16)],
        compiler_params=_cparams(1),
        name="mla_prep",
    )(proj, gq, gkv, wqn, wqr, wk, wv, cos, sin)


def _split3(x):
    hi = x.astype(BF16)
    r1 = x - hi.astype(F32)
    mid = r1.astype(BF16)
    lo = (r1 - mid.astype(F32)).astype(BF16)
    return jnp.concatenate([hi, mid, lo], axis=1)


def _fox_prep_kernel(af_ref, bf_ref, selq_ref, selk_ref, oneq_ref, onek_ref, qx_ref, kx_ref,
                     lf_scr, c_scr):
    seq = af_ref.shape[0]
    x = af_ref[...] + bf_ref[...]
    lf_scr[...] = -(jnp.maximum(-x, 0.0) + jnp.log1p(jnp.exp(-jnp.abs(x))))
    r = lax.broadcasted_iota(jnp.int32, (BLK, BLK), 0)
    c = lax.broadcasted_iota(jnp.int32, (BLK, BLK), 1)
    tri = (c <= r).astype(BF16)

    for t in range(seq // BLK):
        rows = slice(t * BLK, (t + 1) * BLK)
        cs = jnp.dot(tri, _split3(lf_scr[rows, :]), preferred_element_type=F32)
        c_scr[rows, :] = cs[:, :LANES] + cs[:, LANES:2 * LANES] + cs[:, 2 * LANES:]
    totals = [c_scr[t * BLK - 1:t * BLK, :] for t in range(1, seq // BLK)]
    carry = jnp.zeros((1, LANES), F32)
    for t in range(1, seq // BLK):
        carry = carry + totals[t - 1]
        c_scr[t * BLK:(t + 1) * BLK, :] = c_scr[t * BLK:(t + 1) * BLK, :] + carry
    parts = _split3(c_scr[...] * LOG2E)
    qx_ref[...] = (jnp.dot(parts, selq_ref[...], preferred_element_type=F32) + oneq_ref[...]).astype(BF16)
    kx_ref[...] = (jnp.dot(parts, selk_ref[...], preferred_element_type=F32) + onek_ref[...]).astype(BF16)
    row = lax.broadcasted_iota(jnp.int32, (BLK, kx_ref.shape[1]), 0)
    lane = lax.broadcasted_iota(jnp.int32, (BLK, kx_ref.shape[1]), 1)
    pad_bias = (row < PAD) & ((lane & (LANES - 1)) == 2 * FOX_EXT)
    kx_ref[:BLK, :] = jnp.where(pad_bias, NEG, kx_ref[:BLK, :].astype(F32)).astype(BF16)


def _fox_select_constants():
    selq = np.zeros((3 * LANES, 4 * LANES), np.float32)
    selk = np.zeros((3 * LANES, 4 * LANES), np.float32)
    oneq = np.zeros((1, 4 * LANES), np.float32)
    onek = np.zeros((1, 4 * LANES), np.float32)
    for p in range(HEADS // 2):
        for s in range(2):
            h = 2 * p + s
            base = p * LANES + s * FOX_EXT
            for part in range(3):
                selq[part * LANES + h, base + part] = 1.0
                selk[part * LANES + h, base + 3 + part] = -1.0
                oneq[0, base + 3 + part] = 1.0
                onek[0, base + part] = 1.0
        oneq[0, p * LANES + 2 * FOX_EXT] = 1.0
    return (jnp.asarray(selq, BF16), jnp.asarray(selk, BF16), jnp.asarray(oneq), jnp.asarray(onek))


def _fox_prep(af, bf_row, batch, seq):
    selq, selk, oneq, onek = _fox_select_constants()
    full = lambda a: pl.BlockSpec(a.shape, lambda b: (0,) * a.ndim)
    return pl.pallas_call(
        _fox_prep_kernel,
        grid=(batch,),
        in_specs=[pl.BlockSpec((seq, LANES), lambda b: (b, 0)),
                  full(bf_row), full(selq), full(selk), full(oneq), full(onek)],
        out_specs=[pl.BlockSpec((seq, 4 * LANES), lambda b: (b, 0)),
                   pl.BlockSpec((seq, 4 * LANES), lambda b: (b, 0))],
        out_shape=[jax.ShapeDtypeStruct((batch * seq, 4 * LANES), BF16),
                   jax.ShapeDtypeStruct((batch * seq, 4 * LANES), BF16)],
        scratch_shapes=[pltpu.VMEM((seq, LANES), F32), pltpu.VMEM((seq, LANES), F32)],
        compiler_params=_cparams(1),
        name="fox_prep",
    )(af, bf_row, selq, selk, oneq, onek)


def _flash_kernel(qm_ref, qx_ref, km_ref, kx_ref, v_ref, o_ref, vt_scr, qt_scr, m_scr, l_scr, acc_scr,
                  sa_scr, sb_scr, *, scale, ext, kx_per_pair):
    i = pl.program_id(2)
    half = LANES // 2
    nq2 = 2 * TQ
    tile = lambda s: slice(s * LANES, (s + 1) * LANES)
    kx_tile = tile if kx_per_pair else (lambda s: slice(0, LANES))

    @pl.when(i == 0)
    def _():
        for s in range(NS):
            for c in range(v_ref.shape[0] // TQ):
                vt_scr[s, c] = v_ref[c * TQ:(c + 1) * TQ, tile(s)].astype(F32).T.astype(BF16)

    lane = lax.broadcasted_iota(jnp.int32, (TQ, LANES), 1)
    for s in range(NS):
        qm = qm_ref[:, tile(s)].astype(F32) * (scale * LOG2E)
        qx = qx_ref[:, tile(s)].astype(F32)
        shared = lane == 2 * ext
        qa = jnp.concatenate([jnp.where(lane < half, qm, 0.0),
                              jnp.where((lane < ext) | shared, qx, 0.0)], axis=1)
        qb = jnp.concatenate([jnp.where(lane >= half, qm, 0.0),
                              jnp.where(((lane >= ext) & (lane < 2 * ext)) | shared, qx, 0.0)], axis=1)
        for u in range(TQ // BLK):
            rows = slice(u * BLK, (u + 1) * BLK)
            qt_scr[s, :, u * CW:u * CW + BLK] = qa[rows].T.astype(BF16)
            qt_scr[s, :, u * CW + BLK:(u + 1) * CW] = qb[rows].T.astype(BF16)

    m_scr[...] = jnp.full(m_scr.shape, NEG, F32)
    l_scr[...] = jnp.zeros(l_scr.shape, F32)
    acc_scr[...] = jnp.zeros(acc_scr.shape, F32)

    def visible_keys(u, diagonal):
        return (u + 1) * BLK if diagonal else TQ

    def scores(c, s_ref, diagonal=False):
        off = pl.multiple_of(c * TQ, TQ)
        for s in range(NS):
            k = jnp.concatenate([km_ref[pl.ds(off, TQ), tile(s)], kx_ref[pl.ds(off, TQ), kx_tile(s)]],
                                axis=1)
            for u in range(TQ // BLK):
                nk = visible_keys(u, diagonal)
                s_ref[s, :nk, u * CW:(u + 1) * CW] = jnp.dot(k[:nk], qt_scr[s, :, u * CW:(u + 1) * CW],
                                                             preferred_element_type=F32)

    def update(c, s_ref, diagonal):
        for s in range(NS):
            vt = vt_scr[s, c]
            for u in range(TQ // BLK):
                cols = slice(u * CW, (u + 1) * CW)
                nk = visible_keys(u, diagonal)
                st = s_ref[s, :nk, cols]
                if diagonal:
                    key = lax.broadcasted_iota(jnp.int32, (nk, CW), 0)
                    query = u * BLK + (lax.broadcasted_iota(jnp.int32, (nk, CW), 1) & (BLK - 1))
                    st = jnp.where(key <= query, st, NEG)
                m_old = m_scr[s, :, cols]
                m_new = jnp.maximum(m_old, jnp.max(st, axis=0, keepdims=True))
                alpha = jnp.exp2(m_old - m_new)
                pt = jnp.exp2(st - m_new)
                l_scr[s, :, cols] = alpha * l_scr[s, :, cols] + jnp.sum(pt, axis=0, keepdims=True)
                pv = jnp.dot(vt[:, :nk], pt.astype(BF16), preferred_element_type=F32)
                acc_scr[s, :, cols] = alpha * acc_scr[s, :, cols] + pv
                m_scr[s, :, cols] = m_new

    scores(0, sa_scr)

    def pair(j, carry):
        c = 2 * j
        scores(c + 1, sb_scr)
        update(c, sa_scr, False)
        scores(c + 2, sa_scr)
        update(c + 1, sb_scr, False)
        return carry

    npairs = i // 2
    lax.fori_loop(0, npairs, pair, 0)
    odd_tail = i - 2 * npairs == 1

    @pl.when(odd_tail)
    def _():
        scores(i, sb_scr, diagonal=True)
        update(i - 1, sa_scr, False)
        update(i, sb_scr, True)

    @pl.when(jnp.logical_not(odd_tail))
    def _():
        update(i, sa_scr, True)

    feat = lax.broadcasted_iota(jnp.int32, (LANES, BLK), 0)
    for s in range(NS):
        out = acc_scr[s] / l_scr[s]
        for u in range(TQ // BLK):
            both = out[:, u * CW:(u + 1) * CW]
            o_ref[u * BLK:(u + 1) * BLK, tile(s)] = jnp.where(feat < half, both[:, :BLK],
                                                              both[:, BLK:]).T.astype(o_ref.dtype)


def _flash(qm, qm_tile, qx, qx_tile, km, km_tile, kx, kx_tile, kx_per_pair, v, v_tile,
           batch, seq, scale, ext):
    nq = seq // TQ
    groups = HEADS // 2 // NS
    w = NS * LANES
    assert qm_tile % NS == 0 and qx_tile % NS == 0 and km_tile % NS == 0 and v_tile % NS == 0
    if kx_per_pair:
        assert kx_tile % NS == 0
        kx_spec = pl.BlockSpec((seq, w), lambda b, p, i: (b, kx_tile // NS + p))
    else:
        kx_spec = pl.BlockSpec((seq, LANES), lambda b, p, i: (b, kx_tile))
    return pl.pallas_call(
        functools.partial(_flash_kernel, scale=scale, ext=ext, kx_per_pair=kx_per_pair),
        grid=(batch, groups, nq),
        in_specs=[pl.BlockSpec((TQ, w), lambda b, p, i: (b * nq + i, qm_tile // NS + p)),
                  pl.BlockSpec((TQ, w), lambda b, p, i: (b * nq + i, qx_tile // NS + p)),
                  pl.BlockSpec((seq, w), lambda b, p, i: (b, km_tile // NS + p)),
                  kx_spec,
                  pl.BlockSpec((seq, w), lambda b, p, i: (b, v_tile // NS + p))],
        out_specs=pl.BlockSpec((TQ, w), lambda b, p, i: (b * nq + i, p)),
        out_shape=jax.ShapeDtypeStruct((batch * seq, BRANCH_W), BF16),
        scratch_shapes=[pltpu.VMEM((NS, seq // TQ, LANES, TQ), BF16), pltpu.VMEM((NS, 2 * LANES, 2 * TQ), BF16),
                        pltpu.VMEM((NS, 1, 2 * TQ), F32), pltpu.VMEM((NS, 1, 2 * TQ), F32),
                        pltpu.VMEM((NS, LANES, 2 * TQ), F32),
                        pltpu.VMEM((NS, TQ, 2 * TQ), F32), pltpu.VMEM((NS, TQ, 2 * TQ), F32)],
        compiler_params=_cparams(3),
        name="flash",
    )(qm, qx, km, kx, v)


def _swa_kernel(sink_ref, q_ref, kp_ref, kc_ref, vp_ref, vc_ref, cosc_ref, sinc_ref, cosp_ref, sinp_ref,
                o_ref):
    j = pl.program_id(1)
    half = LANES // 2
    rh = HEAD_DIM // 2
    lane = lax.broadcasted_iota(jnp.int32, (BLK, LANES), 1)
    q = _rope(q_ref[...].astype(F32), cosc_ref[...], sinc_ref[...], rh) * (HEAD_DIM ** -0.5 * LOG2E)
    kc = _rope(kc_ref[...].astype(F32), cosc_ref[...], sinc_ref[...], rh).astype(BF16)
    kp = _rope(kp_ref[...].astype(F32), cosp_ref[...], sinp_ref[...], rh).astype(BF16)
    k = jnp.concatenate([kp, kc], axis=0)
    v = jnp.concatenate([vp_ref[...], vc_ref[...]], axis=0).astype(F32)
    pairs = HEADS // 2
    group = lambda t: t // (pairs // SWA_KV_HEADS)
    vts = [v[:, g * LANES:(g + 1) * LANES].T.astype(BF16) for g in range(SWA_KV_HEADS)]

    units = [(r, t) for r in range(SWA_QB) for t in range(pairs)]
    sts = {}
    for r, t in units:
        qt = q[r * BLK:(r + 1) * BLK, t * LANES:(t + 1) * LANES]
        qtt = jnp.concatenate([jnp.where(lane < half, qt, 0.0).T, jnp.where(lane >= half, qt, 0.0).T],
                              axis=1).astype(BF16)
        keys = k[r * BLK:(r + 2) * BLK, group(t) * LANES:(group(t) + 1) * LANES]
        sts[r, t] = jnp.dot(keys, qtt, preferred_element_type=F32)

    col1 = lax.broadcasted_iota(jnp.int32, (1, 2 * BLK), 1)
    feat = lax.broadcasted_iota(jnp.int32, (LANES, BLK), 0)
    kidx = lax.broadcasted_iota(jnp.int32, (2 * BLK, 2 * BLK), 0)
    col = lax.broadcasted_iota(jnp.int32, (2 * BLK, 2 * BLK), 1)
    d = kidx - (col & (BLK - 1))
    in_window = (d >= 1) & (d <= BLK)

    for r in range(SWA_QB):
        n = j * SWA_QB + r
        ok = in_window & (kidx >= PAD - (n - 1) * BLK)
        tiles = []
        for t in range(pairs):
            st = jnp.where(ok, sts[r, t], NEG)
            sink = jnp.where(col1 < BLK, sink_ref[2 * t], sink_ref[2 * t + 1]) * LOG2E
            m = jnp.maximum(jnp.max(st, axis=0, keepdims=True), sink)
            p = jnp.exp2(st - m)
            denom = jnp.sum(p, axis=0, keepdims=True) + jnp.exp2(sink - m)
            vt = vts[group(t)][:, r * BLK:(r + 2) * BLK]
            ot = jnp.dot(vt, p.astype(BF16), preferred_element_type=F32) / denom
            tiles.append(jnp.where(feat < half, ot[:, :BLK], ot[:, BLK:]).T)
        o_ref[r * BLK:(r + 1) * BLK, :] = jnp.concatenate(tiles, axis=1).astype(o_ref.dtype)


def _swa(proj, sinks, cos, sin, batch, seq):
    rows = SWA_QB * BLK
    nt = seq // rows
    kw = 2 * LANES
    prev = lambda j: jnp.maximum(j * SWA_QB - 1, 0)
    cur = lambda w, col: pl.BlockSpec((rows, w), lambda b, j: (b * nt + j, col))
    before = lambda w, col: pl.BlockSpec((BLK, w), lambda b, j: (b * nt * SWA_QB + prev(j), col))
    return pl.pallas_call(
        _swa_kernel,
        grid=(batch, nt),
        in_specs=[pl.BlockSpec(memory_space=pltpu.SMEM),
                  cur(BRANCH_W, (T_CQ * LANES) // BRANCH_W),
                  before(kw, (T_CK * LANES) // kw), cur(kw, (T_CK * LANES) // kw),
                  before(kw, (T_CV * LANES) // kw), cur(kw, (T_CV * LANES) // kw),
                  pl.BlockSpec((rows, LANES), lambda b, j: (j, 0)),
                  pl.BlockSpec((rows, LANES), lambda b, j: (j, 0)),
                  pl.BlockSpec((BLK, LANES), lambda b, j: (prev(j), 0)),
                  pl.BlockSpec((BLK, LANES), lambda b, j: (prev(j), 0))],
        out_specs=pl.BlockSpec((rows, BRANCH_W), lambda b, j: (b * nt + j, 0)),
        out_shape=jax.ShapeDtypeStruct((batch * seq, BRANCH_W), BF16),
        compiler_params=_cparams(2),
        name="swa",
    )(sinks, proj, proj, proj, proj, proj, cos, sin, cos, sin)


def _sigmoid(x):
    return 0.5 * jnp.tanh(0.5 * x) + 0.5


def _out_kernel(ya_ref, yb_ref, yc_ref, za_ref, zb_ref, zc_ref, g_ref, x_ref, head_ref, wb_ref, wo_ref, fg_ref,
                o_ref, *, first, final_norm, tiles_per_seq):
    merged = None
    for n, (y_ref, z_ref) in enumerate(((ya_ref, za_ref), (yb_ref, zb_ref), (yc_ref, zc_ref))):
        z = z_ref[...].astype(F32)
        br = (y_ref[...].astype(F32) * (z * _sigmoid(z))).astype(BF16)
        pr = jnp.dot(br, wb_ref[n], preferred_element_type=F32)
        gate = _sigmoid(g_ref[:, n * D_MODEL:(n + 1) * D_MODEL].astype(F32))
        merged = gate * pr if merged is None else merged + gate * pr
    if first:
        x = _first_layer_rows(x_ref, head_ref, pl.program_id(0) % tiles_per_seq)
    else:
        x = x_ref[...]
    h = x + jnp.dot(merged.astype(BF16), wo_ref[...], preferred_element_type=F32)
    if final_norm:
        h = h * lax.rsqrt(jnp.mean(h * h, axis=-1, keepdims=True) + EPS) * fg_ref[...]
    o_ref[...] = h


def _out(ya, yb, yc, proj, x2d, head, wb, wo, l, fg, batch, seq, first, final):
    assert not (first and final)
    m = batch * seq
    full = lambda a: pl.BlockSpec(a.shape, lambda *_: (0,) * a.ndim)
    zcol = lambda t: (t * LANES) // BRANCH_W
    if final:
        tiles = (seq - BLK) // TM_FINAL
        grid = (batch, tiles)
        row = lambda w, col: pl.BlockSpec((pl.Element(TM_FINAL), pl.Element(w)),
                                          lambda b, i: (pl.multiple_of(b * seq + BLK + i * TM_FINAL, BLK),
                                                        col * w))
        out_spec = pl.BlockSpec((TM_FINAL, D_MODEL), lambda b, i: (b * tiles + i, 0))
        out_rows = batch * (seq - BLK)
    else:
        grid = (m // TM_OUT,)
        row = lambda w, col: pl.BlockSpec((TM_OUT, w), lambda i: (i, col))
        out_spec = row(D_MODEL, 0)
        out_rows = m
    x_spec = _first_layer_spec(TM_OUT, seq, x2d.shape[0] // batch, lambda i: i) if first else row(D_MODEL, 0)
    return pl.pallas_call(
        functools.partial(_out_kernel, first=first, final_norm=final, tiles_per_seq=seq // TM_OUT),
        grid=grid,
        in_specs=[row(BRANCH_W, 0), row(BRANCH_W, 0), row(BRANCH_W, 0),
                  row(BRANCH_W, zcol(T_AZ)), row(BRANCH_W, zcol(T_BZ)), row(BRANCH_W, zcol(T_CZ)),
                  row(N_BRANCH * D_MODEL, 0), x_spec, full(head), _layer_block(wb, l), _layer_block(wo, l),
                  full(fg)],
        out_specs=out_spec,
        out_shape=jax.ShapeDtypeStruct((out_rows, D_MODEL), F32),
        compiler_params=_cparams(len(grid)),
        name="out",
    )(ya, yb, yc, proj, proj, proj, proj, x2d, head, wb, wo, fg)


def _w_in_pieces():
    sizes = (512, 512, 512, 8, 512, MLA_QLORA, MLA_KVLORA, MLA_ROPE, 512, 512, 128, 128, 512, 3072)
    offs = np.concatenate([[0], np.cumsum(sizes)])
    (a_q, a_k, a_v, a_f, a_z, b_cq, b_ckv, b_kr, b_z, c_q, c_k, c_v, c_z, gates) = [int(o) for o in offs[:-1]]
    dst = lambda t: t * LANES
    pieces = [(gates, 3072, dst(T_GATES)), (a_q, 512, dst(T_AQ)), (a_k, 512, dst(T_AK)), (a_v, 512, dst(T_AV)),
              (a_z, 512, dst(T_AZ)), (b_z, 512, dst(T_BZ)), (c_q, 512, dst(T_CQ)), (c_z, 512, dst(T_CZ)),
              (b_cq, MLA_QLORA, dst(T_MLA)), (b_ckv, MLA_KVLORA, dst(T_MLA) + MLA_QLORA),
              (a_f, HEADS, dst(T_AF))]
    for rep in range(LANES // MLA_ROPE):
        pieces.append((b_kr, MLA_ROPE, dst(T_MLA) + MLA_QLORA + MLA_KVLORA + rep * MLA_ROPE))
    for kv in range(SWA_KV_HEADS):
        for rep in range(2):
            col = (2 * kv + rep) * HEAD_DIM
            pieces.append((c_k + kv * HEAD_DIM, HEAD_DIM, dst(T_CK) + col))
            pieces.append((c_v + kv * HEAD_DIM, HEAD_DIM, dst(T_CV) + col))
    return pieces, int(offs[-1])


def _relayout_kernel(w_ref, o_ref):
    o_ref[:, T_AF * LANES:] = jnp.zeros((o_ref.shape[0], N_PROJ - T_AF * LANES), o_ref.dtype)
    for src, width, dst in _w_in_pieces()[0]:
        o_ref[:, dst:dst + width] = w_ref[:, src:src + width].astype(o_ref.dtype)


def _relayout_w_in(w):
    depth, d, n_in = w.shape
    assert n_in == _w_in_pieces()[1]
    rows = 128
    return pl.pallas_call(
        _relayout_kernel,
        grid=(depth, d // rows),
        in_specs=[pl.BlockSpec((None, rows, n_in), lambda l, r: (l, r, 0))],
        out_specs=pl.BlockSpec((None, rows, N_PROJ), lambda l, r: (l, r, 0)),
        out_shape=jax.ShapeDtypeStruct((depth, d, N_PROJ), BF16),
        compiler_params=_cparams(2),
        name="w_in_relayout",
    )(w)


def _relayout_mla_weights(w_uq, w_ukv):
    depth = w_uq.shape[0]
    uq = w_uq.reshape(depth, MLA_QLORA, HEADS, MLA_NOPE + MLA_ROPE)
    wqn = uq[..., :MLA_NOPE].reshape(depth, MLA_QLORA, HEADS * MLA_NOPE)
    rope = uq[..., MLA_NOPE:].reshape(depth, MLA_QLORA, HEADS // 2, 2 * MLA_ROPE)
    wqr = jnp.concatenate([rope, jnp.zeros_like(rope)], axis=-1).reshape(depth, MLA_QLORA, HEADS // 2 * LANES)
    ukv = w_ukv.reshape(depth, MLA_KVLORA, HEADS, MLA_NOPE + MLA_V)
    wk = ukv[..., :MLA_NOPE].reshape(depth, MLA_KVLORA, HEADS * MLA_NOPE)
    wv = ukv[..., MLA_NOPE:].reshape(depth, MLA_KVLORA, HEADS * MLA_V)
    return wqn.astype(BF16), wqr.astype(BF16), wk.astype(BF16), wv.astype(BF16)


def _rope_tables(pos, half):
    inv = ROPE_THETA ** (-jnp.arange(half, dtype=F32) / half)
    ang = pos.astype(F32)[:, None] * inv[None, :]
    cos = jnp.cos(ang)
    sin = jnp.sin(ang)
    reps = LANES // (2 * half)
    cos_full = jnp.tile(jnp.concatenate([cos, cos], axis=1), (1, reps))
    sin_signed = jnp.tile(jnp.concatenate([-sin, sin], axis=1), (1, reps))
    return cos_full, sin_signed


def kernel(x, meta_tokens, norm_g, w_in, b_f, g_cq, g_ckv, w_uq, w_ukv, sinks, w_branch, w_out, final_g):
    batch, seq_in, d = x.shape
    seq = PAD + N_META + seq_in
    depth = w_in.shape[0]
    assert d == D_MODEL and seq % TQ == 0 and seq % TM_PREP == 0
    assert seq % TM_IN == 0 and seq % TM_OUT == 0 and seq_in % TM_FINAL == 0

    head = jnp.concatenate([jnp.zeros((PAD, d), x.dtype), meta_tokens.astype(x.dtype)], axis=0)
    windowed = depth > 1
    if windowed:
        h = x.reshape(batch * seq_in, d)
    else:
        h = jnp.concatenate([jnp.broadcast_to(head[None], (batch, BLK, d)), x], axis=1).reshape(batch * seq, d)
    pos = jnp.arange(seq) - PAD
    cos16, sin16 = _rope_tables(pos, MLA_ROPE // 2)
    cos32, sin32 = _rope_tables(pos, HEAD_DIM // 2)
    fg = final_g.reshape(1, d).astype(F32)

    w = _relayout_w_in(w_in.astype(BF16))
    wqn, wqr, wk, wv = _relayout_mla_weights(w_uq, w_ukv)
    wb = w_branch.astype(BF16)
    wo = w_out.astype(BF16)

    for l in range(depth):
        bf_row = jnp.concatenate([b_f[l].astype(F32), jnp.zeros((LANES - HEADS,), F32)]).reshape(1, LANES)

        first = windowed and l == 0
        proj, af = _inproj(h, head, norm_g[l].reshape(1, d).astype(F32), w, l, batch, seq, first)

        qx, kx = _fox_prep(af, bf_row, batch, seq)
        y_a = _flash(proj, T_AQ, qx, 0, proj, T_AK, kx, 0, True, proj, T_AV,
                     batch, seq, HEAD_DIM ** -0.5, FOX_EXT)

        qn, qr, kn, kr, vb = _mla_prep(proj, g_cq[l].reshape(1, -1).astype(F32),
                                       g_ckv[l].reshape(1, -1).astype(F32),
                                       wqn, wqr, wk, wv, l, cos16, sin16, seq // TM_PREP)
        y_b = _flash(qn, 0, qr, 0, kn, 0, kr, 0, False, vb, 0, batch, seq,
                     (MLA_NOPE + MLA_ROPE) ** -0.5, MLA_EXT)

        y_c = _swa(proj, sinks[l].astype(F32), cos32, sin32, batch, seq)

        h = _out(y_a, y_b, y_c, proj, h, head, wb, wo, l, fg, batch, seq, first, final=(l == depth - 1))

    return h.reshape(batch, seq_in, d)
```

```python
import functools

import numpy as np
import jax
import jax.numpy as jnp
from jax import lax
from jax.experimental import pallas as pl
from jax.experimental.pallas import tpu as pltpu

F32 = jnp.float32
BF16 = jnp.bfloat16

D_MODEL = 1024
N_META = 16
BLK = 128
PAD = BLK - N_META
ROPE_THETA = 10000.0
EPS = 1e-6
NEG = -1e30

HEADS = 8
HEAD_DIM = 64
MLA_NOPE = 64
MLA_ROPE = 32
MLA_V = 64
MLA_QLORA = 384
MLA_KVLORA = 256
SWA_KV_HEADS = 2
BRANCH_W = 512
N_BRANCH = 3

LANES = 128
N_PROJ = 8192
VMEM_LIMIT = 56 * 1024 * 1024

T_GATES = 0
T_AQ = 24
T_AK = 28
T_AV = 32
T_AZ = 36
T_BZ = 40
T_CQ = 44
T_CZ = 48
T_CK = 52
T_MLA = 54
T_CV = 60
T_AF = 62

TM_IN = 1408
TN_IN = 2048
TM_PREP = 1408
TM_OUT = 384
TM_FINAL = 512
TQ = 384
CW = 256
NS = 4
SWA_QB = 3
LOG2E = 1.4426950408889634
FOX_EXT = 6
MLA_EXT = MLA_ROPE


def _cparams(n_axes):
    return pltpu.CompilerParams(dimension_semantics=("arbitrary",) * n_axes,
                                vmem_limit_bytes=VMEM_LIMIT)


def _first_layer_rows(x_ref, head_ref, tile_in_seq):
    xb = x_ref[...]
    shifted = jnp.concatenate([head_ref[...], xb[:xb.shape[0] - BLK]], axis=0)
    return jnp.where(tile_in_seq == 0, shifted, xb)


def _first_layer_spec(tm, seq, seq_in, tile_of):
    tiles = seq // tm

    def index_map(*idx):
        t = tile_of(*idx)
        start = (t // tiles) * seq_in + jnp.maximum((t % tiles) * tm - BLK, 0)
        return pl.multiple_of(start, BLK), 0

    return pl.BlockSpec((pl.Element(tm), pl.Element(D_MODEL)), index_map)


def _inproj_kernel(x_ref, head_ref, g_ref, w_ref, p_ref, af_ref, h_scr, *, first, tiles_per_seq):
    j = pl.program_id(1)

    @pl.when(j == 0)
    def _():
        if first:
            x = _first_layer_rows(x_ref, head_ref, pl.program_id(0) % tiles_per_seq)
        else:
            x = x_ref[...]
        ms = jnp.mean(x * x, axis=-1, keepdims=True)
        h_scr[...] = (x * lax.rsqrt(ms + EPS) * g_ref[...]).astype(BF16)

    acc = jnp.dot(h_scr[...], w_ref[...], preferred_element_type=F32)
    p_ref[...] = acc.astype(BF16)

    @pl.when(j == (T_AF * LANES) // TN_IN)
    def _():
        off = (T_AF * LANES) % TN_IN
        af_ref[...] = acc[:, off:off + LANES]


def _layer_block(a, l):
    return pl.BlockSpec((None,) + a.shape[1:], lambda *_: (l,) + (0,) * (a.ndim - 1))


def _inproj(x2d, head, g, w, l, batch, seq, first):
    m = batch * seq
    if first:
        x_spec = _first_layer_spec(TM_IN, seq, x2d.shape[0] // batch, lambda i, j: i)
    else:
        x_spec = pl.BlockSpec((TM_IN, D_MODEL), lambda i, j: (i, 0))
    return pl.pallas_call(
        functools.partial(_inproj_kernel, first=first, tiles_per_seq=seq // TM_IN),
        grid=(m // TM_IN, N_PROJ // TN_IN),
        in_specs=[x_spec,
                  pl.BlockSpec(head.shape, lambda i, j: (0, 0)),
                  pl.BlockSpec((1, D_MODEL), lambda i, j: (0, 0)),
                  pl.BlockSpec((None, D_MODEL, TN_IN), lambda i, j: (l, 0, j))],
        out_specs=[pl.BlockSpec((TM_IN, TN_IN), lambda i, j: (i, j)),
                   pl.BlockSpec((TM_IN, LANES), lambda i, j: (i, 0))],
        out_shape=[jax.ShapeDtypeStruct((m, N_PROJ), BF16),
                   jax.ShapeDtypeStruct((m, LANES), F32)],
        scratch_shapes=[pltpu.VMEM((TM_IN, D_MODEL), BF16)],
        compiler_params=_cparams(2),
        name="inproj",
    )(x2d, head, g, w)


def _rope(x, cos, sin_signed, half):
    width = x.shape[1]
    reps = width // LANES
    if reps > 1:
        cos = jnp.concatenate([cos] * reps, axis=1)
        sin_signed = jnp.concatenate([sin_signed] * reps, axis=1)
    lane = lax.broadcasted_iota(jnp.int32, x.shape, 1)
    up = pltpu.roll(x, width - half, axis=1)
    down = pltpu.roll(x, half, axis=1)
    swapped = jnp.where((lane & (2 * half - 1)) < half, up, down)
    return x * cos + swapped * sin_signed


def _mla_prep_kernel(p_ref, gq_ref, gkv_ref, wqn_ref, wqr_ref, wk_ref, wv_ref, cos_ref, sin_ref,
                     qn_ref, qr_ref, kn_ref, kr_ref, v_ref, *, seq_tiles):
    blk = p_ref[...].astype(F32)
    cq = blk[:, :MLA_QLORA]
    ckv = blk[:, MLA_QLORA:MLA_QLORA + MLA_KVLORA]
    kr = blk[:, MLA_QLORA + MLA_KVLORA:]
    cq = (cq * lax.rsqrt(jnp.mean(cq * cq, axis=-1, keepdims=True) + EPS) * gq_ref[...]).astype(BF16)
    ckv = (ckv * lax.rsqrt(jnp.mean(ckv * ckv, axis=-1, keepdims=True) + EPS) * gkv_ref[...]).astype(BF16)
    scale = (MLA_NOPE + MLA_ROPE) ** -0.5
    cos = cos_ref[...]
    sin = sin_ref[...]
    qn_ref[...] = jnp.dot(cq, wqn_ref[...], preferred_element_type=F32).astype(BF16)
    qr = jnp.dot(cq, wqr_ref[...], preferred_element_type=F32)
    qr = _rope(qr, cos, sin, MLA_ROPE // 2) * (scale * LOG2E)
    qlane = lax.broadcasted_iota(jnp.int32, qr.shape, 1) & (LANES - 1)
    qr_ref[...] = jnp.where(qlane == 2 * MLA_EXT, 1.0, qr).astype(BF16)
    kn_ref[...] = jnp.dot(ckv, wk_ref[...], preferred_element_type=F32).astype(BF16)
    v_ref[...] = jnp.dot(ckv, wv_ref[...], preferred_element_type=F32).astype(BF16)
    kr = _rope(kr, cos, sin, MLA_ROPE // 2)
    pos_in_seq = (pl.program_id(0) % seq_tiles) * kr.shape[0] + lax.broadcasted_iota(jnp.int32, kr.shape, 0)
    klane = lax.broadcasted_iota(jnp.int32, kr.shape, 1)
    kr_ref[...] = jnp.where(klane == 2 * MLA_EXT, jnp.where(pos_in_seq < PAD, NEG, 0.0), kr).astype(BF16)


def _mla_prep(proj, gq, gkv, wqn, wqr, wk, wv, l, cos, sin, seq_tiles):
    m = proj.shape[0]
    full = lambda a: pl.BlockSpec(a.shape, lambda i: (0,) * a.ndim)
    layer = lambda a: _layer_block(a, l)
    row = lambda w: pl.BlockSpec((TM_PREP, w), lambda i: (i, 0))
    tab = pl.BlockSpec((TM_PREP, LANES), lambda i: (i % seq_tiles, 0))
    mla_w = MLA_QLORA + MLA_KVLORA + LANES
    return pl.pallas_call(
        functools.partial(_mla_prep_kernel, seq_tiles=seq_tiles),
        grid=(m // TM_PREP,),
        in_specs=[pl.BlockSpec((TM_PREP, mla_w), lambda i: (i, (T_MLA * LANES) // mla_w)),
                  full(gq), full(gkv), layer(wqn), layer(wqr), layer(wk), layer(wv), tab, tab],
        out_specs=[row(BRANCH_W), row(BRANCH_W), row(BRANCH_W), row(LANES), row(BRANCH_W)],
        out_shape=[jax.ShapeDtypeStruct((m, BRANCH_W), BF16),
                   jax.ShapeDtypeStruct((m, BRANCH_W), BF16),
                   jax.ShapeDtypeStruct((m, BRANCH_W), BF16),
                   jax.ShapeDtypeStruct((m, LANES), BF16),
                   jax.ShapeDtypeStruct((m, BRANCH_W), BF16)],
        compiler_params=_cparams(1),
        name="mla_prep",
    )(proj, gq, gkv, wqn, wqr, wk, wv, cos, sin)


def _split3(x):
    hi = x.astype(BF16)
    r1 = x - hi.astype(F32)
    mid = r1.astype(BF16)
    lo = (r1 - mid.astype(F32)).astype(BF16)
    return jnp.concatenate([hi, mid, lo], axis=1)


def _fox_prep_kernel(af_ref, bf_ref, selq_ref, selk_ref, oneq_ref, onek_ref, qx_ref, kx_ref,
                     lf_scr, c_scr):
    seq = af_ref.shape[0]
    x = af_ref[...] + bf_ref[...]
    lf_scr[...] = -(jnp.maximum(-x, 0.0) + jnp.log1p(jnp.exp(-jnp.abs(x))))
    r = lax.broadcasted_iota(jnp.int32, (BLK, BLK), 0)
    c = lax.broadcasted_iota(jnp.int32, (BLK, BLK), 1)
    tri = (c <= r).astype(BF16)

    for t in range(seq // BLK):
        rows = slice(t * BLK, (t + 1) * BLK)
        cs = jnp.dot(tri, _split3(lf_scr[rows, :]), preferred_element_type=F32)
        c_scr[rows, :] = cs[:, :LANES] + cs[:, LANES:2 * LANES] + cs[:, 2 * LANES:]
    totals = [c_scr[t * BLK - 1:t * BLK, :] for t in range(1, seq // BLK)]
    carry = jnp.zeros((1, LANES), F32)
    for t in range(1, seq // BLK):
        carry = carry + totals[t - 1]
        c_scr[t * BLK:(t + 1) * BLK, :] = c_scr[t * BLK:(t + 1) * BLK, :] + carry
    parts = _split3(c_scr[...] * LOG2E)
    qx_ref[...] = (jnp.dot(parts, selq_ref[...], preferred_element_type=F32) + oneq_ref[...]).astype(BF16)
    kx_ref[...] = (jnp.dot(parts, selk_ref[...], preferred_element_type=F32) + onek_ref[...]).astype(BF16)
    row = lax.broadcasted_iota(jnp.int32, (BLK, kx_ref.shape[1]), 0)
    lane = lax.broadcasted_iota(jnp.int32, (BLK, kx_ref.shape[1]), 1)
    pad_bias = (row < PAD) & ((lane & (LANES - 1)) == 2 * FOX_EXT)
    kx_ref[:BLK, :] = jnp.where(pad_bias, NEG, kx_ref[:BLK, :].astype(F32)).astype(BF16)


def _fox_select_constants():
    selq = np.zeros((3 * LANES, 4 * LANES), np.float32)
    selk = np.zeros((3 * LANES, 4 * LANES), np.float32)
    oneq = np.zeros((1, 4 * LANES), np.float32)
    onek = np.zeros((1, 4 * LANES), np.float32)
    for p in range(HEADS // 2):
        for s in range(2):
            h = 2 * p + s
            base = p * LANES + s * FOX_EXT
            for part in range(3):
                selq[part * LANES + h, base + part] = 1.0
                selk[part * LANES + h, base + 3 + part] = -1.0
                oneq[0, base + 3 + part] = 1.0
                onek[0, base + part] = 1.0
        oneq[0, p * LANES + 2 * FOX_EXT] = 1.0
    return (jnp.asarray(selq, BF16), jnp.asarray(selk, BF16), jnp.asarray(oneq), jnp.asarray(onek))


def _fox_prep(af, bf_row, batch, seq):
    selq, selk, oneq, onek = _fox_select_constants()
    full = lambda a: pl.BlockSpec(a.shape, lambda b: (0,) * a.ndim)
    return pl.pallas_call(
        _fox_prep_kernel,
        grid=(batch,),
        in_specs=[pl.BlockSpec((seq, LANES), lambda b: (b, 0)),
                  full(bf_row), full(selq), full(selk), full(oneq), full(onek)],
        out_specs=[pl.BlockSpec((seq, 4 * LANES), lambda b: (b, 0)),
                   pl.BlockSpec((seq, 4 * LANES), lambda b: (b, 0))],
        out_shape=[jax.ShapeDtypeStruct((batch * seq, 4 * LANES), BF16),
                   jax.ShapeDtypeStruct((batch * seq, 4 * LANES), BF16)],
        scratch_shapes=[pltpu.VMEM((seq, LANES), F32), pltpu.VMEM((seq, LANES), F32)],
        compiler_params=_cparams(1),
        name="fox_prep",
    )(af, bf_row, selq, selk, oneq, onek)


def _flash_kernel(qm_ref, qx_ref, km_ref, kx_ref, v_ref, o_ref, vt_scr, qt_scr, m_scr, l_scr, acc_scr,
                  sa_scr, sb_scr, *, scale, ext, kx_per_pair):
    i = pl.program_id(2)
    half = LANES // 2
    nq2 = 2 * TQ
    tile = lambda s: slice(s * LANES, (s + 1) * LANES)
    kx_tile = tile if kx_per_pair else (lambda s: slice(0, LANES))

    @pl.when(i == 0)
    def _():
        for s in range(NS):
            for c in range(v_ref.shape[0] // TQ):
                vt_scr[s, c] = v_ref[c * TQ:(c + 1) * TQ, tile(s)].astype(F32).T.astype(BF16)

    lane = lax.broadcasted_iota(jnp.int32, (TQ, LANES), 1)
    for s in range(NS):
        qm = qm_ref[:, tile(s)].astype(F32) * (scale * LOG2E)
        qx = qx_ref[:, tile(s)].astype(F32)
        shared = lane == 2 * ext
        qa = jnp.concatenate([jnp.where(lane < half, qm, 0.0),
                              jnp.where((lane < ext) | shared, qx, 0.0)], axis=1)
        qb = jnp.concatenate([jnp.where(lane >= half, qm, 0.0),
                              jnp.where(((lane >= ext) & (lane < 2 * ext)) | shared, qx, 0.0)], axis=1)
        for u in range(TQ // BLK):
            rows = slice(u * BLK, (u + 1) * BLK)
            qt_scr[s, :, u * CW:u * CW + BLK] = qa[rows].T.astype(BF16)
            qt_scr[s, :, u * CW + BLK:(u + 1) * CW] = qb[rows].T.astype(BF16)

    m_scr[...] = jnp.full(m_scr.shape, NEG, F32)
    l_scr[...] = jnp.zeros(l_scr.shape, F32)
    acc_scr[...] = jnp.zeros(acc_scr.shape, F32)

    def visible_keys(u, diagonal):
        return (u + 1) * BLK if diagonal else TQ

    def scores(c, s_ref, diagonal=False):
        off = pl.multiple_of(c * TQ, TQ)
        for s in range(NS):
            k = jnp.concatenate([km_ref[pl.ds(off, TQ), tile(s)], kx_ref[pl.ds(off, TQ), kx_tile(s)]],
                                axis=1)
            for u in range(TQ // BLK):
                nk = visible_keys(u, diagonal)
                s_ref[s, :nk, u * CW:(u + 1) * CW] = jnp.dot(k[:nk], qt_scr[s, :, u * CW:(u + 1) * CW],
                                                             preferred_element_type=F32)

    def update(c, s_ref, diagonal):
        for s in range(NS):
            vt = vt_scr[s, c]
            for u in range(TQ // BLK):
                cols = slice(u * CW, (u + 1) * CW)
                nk = visible_keys(u, diagonal)
                st = s_ref[s, :nk, cols]
                if diagonal:
                    key = lax.broadcasted_iota(jnp.int32, (nk, CW), 0)
                    query = u * BLK + (lax.broadcasted_iota(jnp.int32, (nk, CW), 1) & (BLK - 1))
                    st = jnp.where(key <= query, st, NEG)
                m_old = m_scr[s, :, cols]
                m_new = jnp.maximum(m_old, jnp.max(st, axis=0, keepdims=True))
                alpha = jnp.exp2(m_old - m_new)
                pt = jnp.exp2(st - m_new)
                l_scr[s, :, cols] = alpha * l_scr[s, :, cols] + jnp.sum(pt, axis=0, keepdims=True)
                pv = jnp.dot(vt[:, :nk], pt.astype(BF16), preferred_element_type=F32)
                acc_scr[s, :, cols] = alpha * acc_scr[s, :, cols] + pv
                m_scr[s, :, cols] = m_new

    scores(0, sa_scr)

    def pair(j, carry):
        c = 2 * j
        scores(c + 1, sb_scr)
        update(c, sa_scr, False)
        scores(c + 2, sa_scr)
        update(c + 1, sb_scr, False)
        return carry

    npairs = i // 2
    lax.fori_loop(0, npairs, pair, 0)
    odd_tail = i - 2 * npairs == 1

    @pl.when(odd_tail)
    def _():
        scores(i, sb_scr, diagonal=True)
        update(i - 1, sa_scr, False)
        update(i, sb_scr, True)

    @pl.when(jnp.logical_not(odd_tail))
    def _():
        update(i, sa_scr, True)

    feat = lax.broadcasted_iota(jnp.int32, (LANES, BLK), 0)
    for s in range(NS):
        out = acc_scr[s] / l_scr[s]
        for u in range(TQ // BLK):
            both = out[:, u * CW:(u + 1) * CW]
            o_ref[u * BLK:(u + 1) * BLK, tile(s)] = jnp.where(feat < half, both[:, :BLK],
                                                              both[:, BLK:]).T.astype(o_ref.dtype)


def _flash(qm, qm_tile, qx, qx_tile, km, km_tile, kx, kx_tile, kx_per_pair, v, v_tile,
           batch, seq, scale, ext):
    nq = seq // TQ
    groups = HEADS // 2 // NS
    w = NS * LANES
    assert qm_tile % NS == 0 and qx_tile % NS == 0 and km_tile % NS == 0 and v_tile % NS == 0
    if kx_per_pair:
        assert kx_tile % NS == 0
        kx_spec = pl.BlockSpec((seq, w), lambda b, p, i: (b, kx_tile // NS + p))
    else:
        kx_spec = pl.BlockSpec((seq, LANES), lambda b, p, i: (b, kx_tile))
    return pl.pallas_call(
        functools.partial(_flash_kernel, scale=scale, ext=ext, kx_per_pair=kx_per_pair),
        grid=(batch, groups, nq),
        in_specs=[pl.BlockSpec((TQ, w), lambda b, p, i: (b * nq + i, qm_tile // NS + p)),
                  pl.BlockSpec((TQ, w), lambda b, p, i: (b * nq + i, qx_tile // NS + p)),
                  pl.BlockSpec((seq, w), lambda b, p, i: (b, km_tile // NS + p)),
                  kx_spec,
                  pl.BlockSpec((seq, w), lambda b, p, i: (b, v_tile // NS + p))],
        out_specs=pl.BlockSpec((TQ, w), lambda b, p, i: (b * nq + i, p)),
        out_shape=jax.ShapeDtypeStruct((batch * seq, BRANCH_W), BF16),
        scratch_shapes=[pltpu.VMEM((NS, seq // TQ, LANES, TQ), BF16), pltpu.VMEM((NS, 2 * LANES, 2 * TQ), BF16),
                        pltpu.VMEM((NS, 1, 2 * TQ), F32), pltpu.VMEM((NS, 1, 2 * TQ), F32),
                        pltpu.VMEM((NS, LANES, 2 * TQ), F32),
                        pltpu.VMEM((NS, TQ, 2 * TQ), F32), pltpu.VMEM((NS, TQ, 2 * TQ), F32)],
        compiler_params=_cparams(3),
        name="flash",
    )(qm, qx, km, kx, v)


def _swa_kernel(sink_ref, q_ref, kp_ref, kc_ref, vp_ref, vc_ref, cosc_ref, sinc_ref, cosp_ref, sinp_ref,
                o_ref):
    j = pl.program_id(1)
    half = LANES // 2
    rh = HEAD_DIM // 2
    lane = lax.broadcasted_iota(jnp.int32, (BLK, LANES), 1)
    q = _rope(q_ref[...].astype(F32), cosc_ref[...], sinc_ref[...], rh) * (HEAD_DIM ** -0.5 * LOG2E)
    kc = _rope(kc_ref[...].astype(F32), cosc_ref[...], sinc_ref[...], rh).astype(BF16)
    kp = _rope(kp_ref[...].astype(F32), cosp_ref[...], sinp_ref[...], rh).astype(BF16)
    k = jnp.concatenate([kp, kc], axis=0)
    v = jnp.concatenate([vp_ref[...], vc_ref[...]], axis=0).astype(F32)
    pairs = HEADS // 2
    group = lambda t: t // (pairs // SWA_KV_HEADS)
    vts = [v[:, g * LANES:(g + 1) * LANES].T.astype(BF16) for g in range(SWA_KV_HEADS)]

    units = [(r, t) for r in range(SWA_QB) for t in range(pairs)]
    sts = {}
    for r, t in units:
        qt = q[r * BLK:(r + 1) * BLK, t * LANES:(t + 1) * LANES]
        qtt = jnp.concatenate([jnp.where(lane < half, qt, 0.0).T, jnp.where(lane >= half, qt, 0.0).T],
                              axis=1).astype(BF16)
        keys = k[r * BLK:(r + 2) * BLK, group(t) * LANES:(group(t) + 1) * LANES]
        sts[r, t] = jnp.dot(keys, qtt, preferred_element_type=F32)

    col1 = lax.broadcasted_iota(jnp.int32, (1, 2 * BLK), 1)
    feat = lax.broadcasted_iota(jnp.int32, (LANES, BLK), 0)
    kidx = lax.broadcasted_iota(jnp.int32, (2 * BLK, 2 * BLK), 0)
    col = lax.broadcasted_iota(jnp.int32, (2 * BLK, 2 * BLK), 1)
    d = kidx - (col & (BLK - 1))
    in_window = (d >= 1) & (d <= BLK)

    for r in range(SWA_QB):
        n = j * SWA_QB + r
        ok = in_window & (kidx >= PAD - (n - 1) * BLK)
        tiles = []
        for t in range(pairs):
            st = jnp.where(ok, sts[r, t], NEG)
            sink = jnp.where(col1 < BLK, sink_ref[2 * t], sink_ref[2 * t + 1]) * LOG2E
            m = jnp.maximum(jnp.max(st, axis=0, keepdims=True), sink)
            p = jnp.exp2(st - m)
            denom = jnp.sum(p, axis=0, keepdims=True) + jnp.exp2(sink - m)
            vt = vts[group(t)][:, r * BLK:(r + 2) * BLK]
            ot = jnp.dot(vt, p.astype(BF16), preferred_element_type=F32) / denom
            tiles.append(jnp.where(feat < half, ot[:, :BLK], ot[:, BLK:]).T)
        o_ref[r * BLK:(r + 1) * BLK, :] = jnp.concatenate(tiles, axis=1).astype(o_ref.dtype)


def _swa(proj, sinks, cos, sin, batch, seq):
    rows = SWA_QB * BLK
    nt = seq // rows
    kw = 2 * LANES
    prev = lambda j: jnp.maximum(j * SWA_QB - 1, 0)
    cur = lambda w, col: pl.BlockSpec((rows, w), lambda b, j: (b * nt + j, col))
    before = lambda w, col: pl.BlockSpec((BLK, w), lambda b, j: (b * nt * SWA_QB + prev(j), col))
    return pl.pallas_call(
        _swa_kernel,
        grid=(batch, nt),
        in_specs=[pl.BlockSpec(memory_space=pltpu.SMEM),
                  cur(BRANCH_W, (T_CQ * LANES) // BRANCH_W),
                  before(kw, (T_CK * LANES) // kw), cur(kw, (T_CK * LANES) // kw),
                  before(kw, (T_CV * LANES) // kw), cur(kw, (T_CV * LANES) // kw),
                  pl.BlockSpec((rows, LANES), lambda b, j: (j, 0)),
                  pl.BlockSpec((rows, LANES), lambda b, j: (j, 0)),
                  pl.BlockSpec((BLK, LANES), lambda b, j: (prev(j), 0)),
                  pl.BlockSpec((BLK, LANES), lambda b, j: (prev(j), 0))],
        out_specs=pl.BlockSpec((rows, BRANCH_W), lambda b, j: (b * nt + j, 0)),
        out_shape=jax.ShapeDtypeStruct((batch * seq, BRANCH_W), BF16),
        compiler_params=_cparams(2),
        name="swa",
    )(sinks, proj, proj, proj, proj, proj, cos, sin, cos, sin)


def _out_kernel(ya_ref, yb_ref, yc_ref, za_ref, zb_ref, zc_ref, g_ref, x_ref, head_ref, wb_ref, wo_ref, fg_ref,
                o_ref, *, first, final_norm, tiles_per_seq):
    merged = None
    for n, (y_ref, z_ref) in enumerate(((ya_ref, za_ref), (yb_ref, zb_ref), (yc_ref, zc_ref))):
        zh = z_ref[...].astype(F32)
        br = (y_ref[...].astype(F32) * (zh * (1.0 + jnp.tanh(zh)))).astype(BF16)
        pr = jnp.dot(br, wb_ref[n], preferred_element_type=F32)
        gate2 = 1.0 + jnp.tanh(g_ref[:, n * D_MODEL:(n + 1) * D_MODEL].astype(F32))
        merged = gate2 * pr if merged is None else merged + gate2 * pr
    if first:
        x = _first_layer_rows(x_ref, head_ref, pl.program_id(0) % tiles_per_seq)
    else:
        x = x_ref[...]
    h = x + jnp.dot(merged.astype(BF16), wo_ref[...], preferred_element_type=F32)
    if final_norm:
        h = h * lax.rsqrt(jnp.mean(h * h, axis=-1, keepdims=True) + EPS) * fg_ref[...]
    o_ref[...] = h


def _out(ya, yb, yc, proj, x2d, head, wb, wo, l, fg, batch, seq, first, final):
    assert not (first and final)
    m = batch * seq
    full = lambda a: pl.BlockSpec(a.shape, lambda *_: (0,) * a.ndim)
    zcol = lambda t: (t * LANES) // BRANCH_W
    if final:
        tiles = (seq - BLK) // TM_FINAL
        grid = (batch, tiles)
        row = lambda w, col: pl.BlockSpec((pl.Element(TM_FINAL), pl.Element(w)),
                                          lambda b, i: (pl.multiple_of(b * seq + BLK + i * TM_FINAL, BLK),
                                                        col * w))
        out_spec = pl.BlockSpec((TM_FINAL, D_MODEL), lambda b, i: (b * tiles + i, 0))
        out_rows = batch * (seq - BLK)
    else:
        grid = (m // TM_OUT,)
        row = lambda w, col: pl.BlockSpec((TM_OUT, w), lambda i: (i, col))
        out_spec = row(D_MODEL, 0)
        out_rows = m
    x_spec = _first_layer_spec(TM_OUT, seq, x2d.shape[0] // batch, lambda i: i) if first else row(D_MODEL, 0)
    return pl.pallas_call(
        functools.partial(_out_kernel, first=first, final_norm=final, tiles_per_seq=seq // TM_OUT),
        grid=grid,
        in_specs=[row(BRANCH_W, 0), row(BRANCH_W, 0), row(BRANCH_W, 0),
                  row(BRANCH_W, zcol(T_AZ)), row(BRANCH_W, zcol(T_BZ)), row(BRANCH_W, zcol(T_CZ)),
                  row(N_BRANCH * D_MODEL, 0), x_spec, full(head), _layer_block(wb, l), _layer_block(wo, l),
                  full(fg)],
        out_specs=out_spec,
        out_shape=jax.ShapeDtypeStruct((out_rows, D_MODEL), F32),
        compiler_params=_cparams(len(grid)),
        name="out",
    )(ya, yb, yc, proj, proj, proj, proj, x2d, head, wb, wo, fg)


def _w_in_pieces():
    sizes = (512, 512, 512, 8, 512, MLA_QLORA, MLA_KVLORA, MLA_ROPE, 512, 512, 128, 128, 512, 3072)
    offs = np.concatenate([[0], np.cumsum(sizes)])
    (a_q, a_k, a_v, a_f, a_z, b_cq, b_ckv, b_kr, b_z, c_q, c_k, c_v, c_z, gates) = [int(o) for o in offs[:-1]]
    dst = lambda t: t * LANES
    pieces = [(gates, 3072, dst(T_GATES), 0.5), (a_q, 512, dst(T_AQ), 1.0), (a_k, 512, dst(T_AK), 1.0),
              (a_v, 512, dst(T_AV), 1.0), (a_z, 512, dst(T_AZ), 0.5), (b_z, 512, dst(T_BZ), 0.5),
              (c_q, 512, dst(T_CQ), 1.0), (c_z, 512, dst(T_CZ), 0.5),
              (b_cq, MLA_QLORA, dst(T_MLA), 1.0), (b_ckv, MLA_KVLORA, dst(T_MLA) + MLA_QLORA, 1.0),
              (a_f, HEADS, dst(T_AF), 1.0)]
    for rep in range(LANES // MLA_ROPE):
        pieces.append((b_kr, MLA_ROPE, dst(T_MLA) + MLA_QLORA + MLA_KVLORA + rep * MLA_ROPE, 1.0))
    for kv in range(SWA_KV_HEADS):
        for rep in range(2):
            col = (2 * kv + rep) * HEAD_DIM
            pieces.append((c_k + kv * HEAD_DIM, HEAD_DIM, dst(T_CK) + col, 1.0))
            pieces.append((c_v + kv * HEAD_DIM, HEAD_DIM, dst(T_CV) + col, 1.0))
    return pieces, int(offs[-1])


def _relayout_kernel(w_ref, o_ref):
    o_ref[:, T_AF * LANES:] = jnp.zeros((o_ref.shape[0], N_PROJ - T_AF * LANES), o_ref.dtype)
    for src, width, dst, scale in _w_in_pieces()[0]:
        piece = w_ref[:, src:src + width]
        if scale != 1.0:
            piece = piece * scale
        o_ref[:, dst:dst + width] = piece.astype(o_ref.dtype)


def _relayout_w_in(w):
    depth, d, n_in = w.shape
    assert n_in == _w_in_pieces()[1]
    rows = 128
    return pl.pallas_call(
        _relayout_kernel,
        grid=(depth, d // rows),
        in_specs=[pl.BlockSpec((None, rows, n_in), lambda l, r: (l, r, 0))],
        out_specs=pl.BlockSpec((None, rows, N_PROJ), lambda l, r: (l, r, 0)),
        out_shape=jax.ShapeDtypeStruct((depth, d, N_PROJ), BF16),
        compiler_params=_cparams(2),
        name="w_in_relayout",
    )(w)


def _relayout_mla_weights(w_uq, w_ukv):
    depth = w_uq.shape[0]
    uq = w_uq.reshape(depth, MLA_QLORA, HEADS, MLA_NOPE + MLA_ROPE)
    wqn = uq[..., :MLA_NOPE].reshape(depth, MLA_QLORA, HEADS * MLA_NOPE)
    rope = uq[..., MLA_NOPE:].reshape(depth, MLA_QLORA, HEADS // 2, 2 * MLA_ROPE)
    wqr = jnp.concatenate([rope, jnp.zeros_like(rope)], axis=-1).reshape(depth, MLA_QLORA, HEADS // 2 * LANES)
    ukv = w_ukv.reshape(depth, MLA_KVLORA, HEADS, MLA_NOPE + MLA_V)
    wk = ukv[..., :MLA_NOPE].reshape(depth, MLA_KVLORA, HEADS * MLA_NOPE)
    wv = ukv[..., MLA_NOPE:].reshape(depth, MLA_KVLORA, HEADS * MLA_V)
    return wqn.astype(BF16), wqr.astype(BF16), wk.astype(BF16), wv.astype(BF16)


def _rope_tables(pos, half):
    inv = ROPE_THETA ** (-jnp.arange(half, dtype=F32) / half)
    ang = pos.astype(F32)[:, None] * inv[None, :]
    cos = jnp.cos(ang)
    sin = jnp.sin(ang)
    reps = LANES // (2 * half)
    cos_full = jnp.tile(jnp.concatenate([cos, cos], axis=1), (1, reps))
    sin_signed = jnp.tile(jnp.concatenate([-sin, sin], axis=1), (1, reps))
    return cos_full, sin_signed


def kernel(x, meta_tokens, norm_g, w_in, b_f, g_cq, g_ckv, w_uq, w_ukv, sinks, w_branch, w_out, final_g):
    batch, seq_in, d = x.shape
    seq = PAD + N_META + seq_in
    depth = w_in.shape[0]
    assert d == D_MODEL and seq % TQ == 0 and seq % TM_PREP == 0
    assert seq % TM_IN == 0 and seq % TM_OUT == 0 and seq_in % TM_FINAL == 0

    head = jnp.concatenate([jnp.zeros((PAD, d), x.dtype), meta_tokens.astype(x.dtype)], axis=0)
    windowed = depth > 1
    if windowed:
        h = x.reshape(batch * seq_in, d)
    else:
        h = jnp.concatenate([jnp.broadcast_to(head[None], (batch, BLK, d)), x], axis=1).reshape(batch * seq, d)
    pos = jnp.arange(seq) - PAD
    cos16, sin16 = _rope_tables(pos, MLA_ROPE // 2)
    cos32, sin32 = _rope_tables(pos, HEAD_DIM // 2)
    fg = final_g.reshape(1, d).astype(F32)

    w = _relayout_w_in(w_in)
    wqn, wqr, wk, wv = _relayout_mla_weights(w_uq, w_ukv)
    wb = w_branch.astype(BF16)
    wo = (0.5 * w_out).astype(BF16)

    for l in range(depth):
        bf_row = jnp.concatenate([b_f[l].astype(F32), jnp.zeros((LANES - HEADS,), F32)]).reshape(1, LANES)

        first = windowed and l == 0
        proj, af = _inproj(h, head, norm_g[l].reshape(1, d).astype(F32), w, l, batch, seq, first)

        qx, kx = _fox_prep(af, bf_row, batch, seq)
        y_a = _flash(proj, T_AQ, qx, 0, proj, T_AK, kx, 0, True, proj, T_AV,
                     batch, seq, HEAD_DIM ** -0.5, FOX_EXT)

        qn, qr, kn, kr, vb = _mla_prep(proj, g_cq[l].reshape(1, -1).astype(F32),
                                       g_ckv[l].reshape(1, -1).astype(F32),
                                       wqn, wqr, wk, wv, l, cos16, sin16, seq // TM_PREP)
        y_b = _flash(qn, 0, qr, 0, kn, 0, kr, 0, False, vb, 0, batch, seq,
                     (MLA_NOPE + MLA_ROPE) ** -0.5, MLA_EXT)

        y_c = _swa(proj, sinks[l].astype(F32), cos32, sin32, batch, seq)

        h = _out(y_a, y_b, y_c, proj, h, head, wb, wo, l, fg, batch, seq, first, final=(l == depth - 1))

    return h.reshape(batch, seq_in, d)
```

```python
import functools

import numpy as np
import jax
import jax.numpy as jnp
from jax import lax
from jax.experimental import pallas as pl
from jax.experimental.pallas import tpu as pltpu

F32 = jnp.float32
BF16 = jnp.bfloat16

D_MODEL = 1024
N_META = 16
BLK = 128
PAD = BLK - N_META
ROPE_THETA = 10000.0
EPS = 1e-6
NEG = -1e30

HEADS = 8
HEAD_DIM = 64
MLA_NOPE = 64
MLA_ROPE = 32
MLA_V = 64
MLA_QLORA = 384
MLA_KVLORA = 256
SWA_KV_HEADS = 2
BRANCH_W = 512
N_BRANCH = 3

LANES = 128
N_PROJ = 7680
VMEM_LIMIT = 56 * 1024 * 1024

T_GATES = 0
T_AQ = 24
T_AK = 28
T_AV = 32
T_AZ = 36
T_BZ = 40
T_CQ = 44
T_CZ = 48
T_CK = 52
T_CV = 53
T_MLA = 54
T_AF = 59
AF_LANE = 96

TM_IN = 1408
TN_IN = 1536
TM_PREP = 1408
TM_OUT = 384
TM_FINAL = 512
TQ = 384
CW = 256
NS = 4
SWA_QB = 3
LOG2E = 1.4426950408889634
FOX_EXT = 6
MLA_EXT = MLA_ROPE


def _cparams(n_axes):
    return pltpu.CompilerParams(dimension_semantics=("arbitrary",) * n_axes,
                                vmem_limit_bytes=VMEM_LIMIT)


def _first_layer_rows(x_ref, head_ref, tile_in_seq):
    xb = x_ref[...]
    shifted = jnp.concatenate([head_ref[...], xb[:xb.shape[0] - BLK]], axis=0)
    return jnp.where(tile_in_seq == 0, shifted, xb)


def _first_layer_spec(tm, seq, seq_in, tile_of):
    tiles = seq // tm

    def index_map(*idx):
        t = tile_of(*idx)
        start = (t // tiles) * seq_in + jnp.maximum((t % tiles) * tm - BLK, 0)
        return pl.multiple_of(start, BLK), 0

    return pl.BlockSpec((pl.Element(tm), pl.Element(D_MODEL)), index_map)


def _inproj_kernel(x_ref, head_ref, g_ref, w_ref, p_ref, af_ref, h_scr, *, first, tiles_per_seq):
    j = pl.program_id(1)

    @pl.when(j == 0)
    def _():
        if first:
            x = _first_layer_rows(x_ref, head_ref, pl.program_id(0) % tiles_per_seq)
        else:
            x = x_ref[...]
        ms = jnp.mean(x * x, axis=-1, keepdims=True)
        h_scr[...] = (x * lax.rsqrt(ms + EPS) * g_ref[...]).astype(BF16)

    acc = jnp.dot(h_scr[...], w_ref[...], preferred_element_type=F32)
    p_ref[...] = acc.astype(BF16)

    @pl.when(j == (T_AF * LANES) // TN_IN)
    def _():
        off = (T_AF * LANES) % TN_IN
        af_ref[...] = acc[:, off:off + LANES]


def _layer_block(a, l):
    return pl.BlockSpec((None,) + a.shape[1:], lambda *_: (l,) + (0,) * (a.ndim - 1))


def _inproj(x2d, head, g, w, l, batch, seq, first):
    m = batch * seq
    if first:
        x_spec = _first_layer_spec(TM_IN, seq, x2d.shape[0] // batch, lambda i, j: i)
    else:
        x_spec = pl.BlockSpec((TM_IN, D_MODEL), lambda i, j: (i, 0))
    return pl.pallas_call(
        functools.partial(_inproj_kernel, first=first, tiles_per_seq=seq // TM_IN),
        grid=(m // TM_IN, N_PROJ // TN_IN),
        in_specs=[x_spec,
                  pl.BlockSpec(head.shape, lambda i, j: (0, 0)),
                  pl.BlockSpec((1, D_MODEL), lambda i, j: (0, 0)),
                  pl.BlockSpec((None, D_MODEL, TN_IN), lambda i, j: (l, 0, j))],
        out_specs=[pl.BlockSpec((TM_IN, TN_IN), lambda i, j: (i, j)),
                   pl.BlockSpec((TM_IN, LANES), lambda i, j: (i, 0))],
        out_shape=[jax.ShapeDtypeStruct((m, N_PROJ), BF16),
                   jax.ShapeDtypeStruct((m, LANES), F32)],
        scratch_shapes=[pltpu.VMEM((TM_IN, D_MODEL), BF16)],
        compiler_params=_cparams(2),
        name="inproj",
    )(x2d, head, g, w)


def _rope(x, cos, sin_signed, half):
    width = x.shape[1]
    reps = width // LANES
    if reps > 1:
        cos = jnp.concatenate([cos] * reps, axis=1)
        sin_signed = jnp.concatenate([sin_signed] * reps, axis=1)
    lane = lax.broadcasted_iota(jnp.int32, x.shape, 1)
    up = pltpu.roll(x, width - half, axis=1)
    down = pltpu.roll(x, half, axis=1)
    swapped = jnp.where((lane & (2 * half - 1)) < half, up, down)
    return x * cos + swapped * sin_signed


def _mla_prep_kernel(p_ref, gq_ref, gkv_ref, wqn_ref, wqr_ref, wk_ref, wv_ref, cos_ref, sin_ref,
                     qn_ref, qr_ref, kn_ref, kr_ref, v_ref, *, seq_tiles):
    blk = p_ref[...].astype(F32)
    cq = blk[:, :MLA_QLORA]
    ckv = blk[:, MLA_QLORA:MLA_QLORA + MLA_KVLORA]
    kr = blk[:, MLA_QLORA + MLA_KVLORA:]
    cq = (cq * lax.rsqrt(jnp.mean(cq * cq, axis=-1, keepdims=True) + EPS) * gq_ref[...]).astype(BF16)
    ckv = (ckv * lax.rsqrt(jnp.mean(ckv * ckv, axis=-1, keepdims=True) + EPS) * gkv_ref[...]).astype(BF16)
    scale = (MLA_NOPE + MLA_ROPE) ** -0.5
    cos = cos_ref[...]
    sin = sin_ref[...]
    qn_ref[...] = jnp.dot(cq, wqn_ref[...], preferred_element_type=F32).astype(BF16)
    qr = jnp.dot(cq, wqr_ref[...], preferred_element_type=F32)
    qr = _rope(qr, cos, sin, MLA_ROPE // 2) * (scale * LOG2E)
    qlane = lax.broadcasted_iota(jnp.int32, qr.shape, 1) & (LANES - 1)
    qr_ref[...] = jnp.where(qlane == 2 * MLA_EXT, 1.0, qr).astype(BF16)
    kn_ref[...] = jnp.dot(ckv, wk_ref[...], preferred_element_type=F32).astype(BF16)
    v_ref[...] = jnp.dot(ckv, wv_ref[...], preferred_element_type=F32).astype(BF16)
    kr = _rope(kr, cos, sin, MLA_ROPE // 2)
    pos_in_seq = (pl.program_id(0) % seq_tiles) * kr.shape[0] + lax.broadcasted_iota(jnp.int32, kr.shape, 0)
    klane = lax.broadcasted_iota(jnp.int32, kr.shape, 1)
    kr_ref[...] = jnp.where(klane == 2 * MLA_EXT, jnp.where(pos_in_seq < PAD, NEG, 0.0), kr).astype(BF16)


def _mla_prep(proj, gq, gkv, wqn, wqr, wk, wv, l, cos, sin, seq_tiles):
    m = proj.shape[0]
    full = lambda a: pl.BlockSpec(a.shape, lambda i: (0,) * a.ndim)
    layer = lambda a: _layer_block(a, l)
    row = lambda w: pl.BlockSpec((TM_PREP, w), lambda i: (i, 0))
    tab = pl.BlockSpec((TM_PREP, LANES), lambda i: (i % seq_tiles, 0))
    mla_w = MLA_QLORA + MLA_KVLORA + LANES
    return pl.pallas_call(
        functools.partial(_mla_prep_kernel, seq_tiles=seq_tiles),
        grid=(m // TM_PREP,),
        in_specs=[pl.BlockSpec((TM_PREP, mla_w), lambda i: (i, (T_MLA * LANES) // mla_w)),
                  full(gq), full(gkv), layer(wqn), layer(wqr), layer(wk), layer(wv), tab, tab],
        out_specs=[row(BRANCH_W), row(BRANCH_W), row(BRANCH_W), row(LANES), row(BRANCH_W)],
        out_shape=[jax.ShapeDtypeStruct((m, BRANCH_W), BF16),
                   jax.ShapeDtypeStruct((m, BRANCH_W), BF16),
                   jax.ShapeDtypeStruct((m, BRANCH_W), BF16),
                   jax.ShapeDtypeStruct((m, LANES), BF16),
                   jax.ShapeDtypeStruct((m, BRANCH_W), BF16)],
        compiler_params=_cparams(1),
        name="mla_prep",
    )(proj, gq, gkv, wqn, wqr, wk, wv, cos, sin)


def _split3(x):
    hi = x.astype(BF16)
    r1 = x - hi.astype(F32)
    mid = r1.astype(BF16)
    lo = (r1 - mid.astype(F32)).astype(BF16)
    return jnp.concatenate([hi, mid, lo], axis=1)


def _fox_prep_kernel(af_ref, bf_ref, selq_ref, selk_ref, oneq_ref, onek_ref, qx_ref, kx_ref,
                     lf_scr, c_scr):
    seq = af_ref.shape[0]
    x = af_ref[...] + bf_ref[...]
    lf_scr[...] = -(jnp.maximum(-x, 0.0) + jnp.log1p(jnp.exp(-jnp.abs(x))))
    r = lax.broadcasted_iota(jnp.int32, (BLK, BLK), 0)
    c = lax.broadcasted_iota(jnp.int32, (BLK, BLK), 1)
    tri = (c <= r).astype(BF16)

    for t in range(seq // BLK):
        rows = slice(t * BLK, (t + 1) * BLK)
        cs = jnp.dot(tri, _split3(lf_scr[rows, :]), preferred_element_type=F32)
        c_scr[rows, :] = cs[:, :LANES] + cs[:, LANES:2 * LANES] + cs[:, 2 * LANES:]
    totals = [c_scr[t * BLK - 1:t * BLK, :] for t in range(1, seq // BLK)]
    carry = jnp.zeros((1, LANES), F32)
    for t in range(1, seq // BLK):
        carry = carry + totals[t - 1]
        c_scr[t * BLK:(t + 1) * BLK, :] = c_scr[t * BLK:(t + 1) * BLK, :] + carry
    parts = _split3(c_scr[...] * LOG2E)
    qx_ref[...] = (jnp.dot(parts, selq_ref[...], preferred_element_type=F32) + oneq_ref[...]).astype(BF16)
    kx_ref[...] = (jnp.dot(parts, selk_ref[...], preferred_element_type=F32) + onek_ref[...]).astype(BF16)
    row = lax.broadcasted_iota(jnp.int32, (BLK, kx_ref.shape[1]), 0)
    lane = lax.broadcasted_iota(jnp.int32, (BLK, kx_ref.shape[1]), 1)
    pad_bias = (row < PAD) & ((lane & (LANES - 1)) == 2 * FOX_EXT)
    kx_ref[:BLK, :] = jnp.where(pad_bias, NEG, kx_ref[:BLK, :].astype(F32)).astype(BF16)


def _fox_select_constants():
    selq = np.zeros((3 * LANES, 4 * LANES), np.float32)
    selk = np.zeros((3 * LANES, 4 * LANES), np.float32)
    oneq = np.zeros((1, 4 * LANES), np.float32)
    onek = np.zeros((1, 4 * LANES), np.float32)
    for p in range(HEADS // 2):
        for s in range(2):
            h = 2 * p + s
            base = p * LANES + s * FOX_EXT
            for part in range(3):
                selq[part * LANES + AF_LANE + h, base + part] = 1.0
                selk[part * LANES + AF_LANE + h, base + 3 + part] = -1.0
                oneq[0, base + 3 + part] = 1.0
                onek[0, base + part] = 1.0
        oneq[0, p * LANES + 2 * FOX_EXT] = 1.0
    return (jnp.asarray(selq, BF16), jnp.asarray(selk, BF16), jnp.asarray(oneq), jnp.asarray(onek))


def _fox_prep(af, bf_row, batch, seq):
    selq, selk, oneq, onek = _fox_select_constants()
    full = lambda a: pl.BlockSpec(a.shape, lambda b: (0,) * a.ndim)
    return pl.pallas_call(
        _fox_prep_kernel,
        grid=(batch,),
        in_specs=[pl.BlockSpec((seq, LANES), lambda b: (b, 0)),
                  full(bf_row), full(selq), full(selk), full(oneq), full(onek)],
        out_specs=[pl.BlockSpec((seq, 4 * LANES), lambda b: (b, 0)),
                   pl.BlockSpec((seq, 4 * LANES), lambda b: (b, 0))],
        out_shape=[jax.ShapeDtypeStruct((batch * seq, 4 * LANES), BF16),
                   jax.ShapeDtypeStruct((batch * seq, 4 * LANES), BF16)],
        scratch_shapes=[pltpu.VMEM((seq, LANES), F32), pltpu.VMEM((seq, LANES), F32)],
        compiler_params=_cparams(1),
        name="fox_prep",
    )(af, bf_row, selq, selk, oneq, onek)


def _flash_kernel(qm_ref, qx_ref, km_ref, kx_ref, v_ref, o_ref, vt_scr, qt_scr, m_scr, l_scr, acc_scr,
                  sa_scr, sb_scr, *, scale, ext, kx_per_pair):
    i = pl.program_id(2)
    half = LANES // 2
    nq2 = 2 * TQ
    tile = lambda s: slice(s * LANES, (s + 1) * LANES)
    kx_tile = tile if kx_per_pair else (lambda s: slice(0, LANES))

    @pl.when(i == 0)
    def _():
        for s in range(NS):
            for c in range(v_ref.shape[0] // TQ):
                vt_scr[s, c] = v_ref[c * TQ:(c + 1) * TQ, tile(s)].astype(F32).T.astype(BF16)

    lane = lax.broadcasted_iota(jnp.int32, (TQ, LANES), 1)
    for s in range(NS):
        qm = qm_ref[:, tile(s)].astype(F32) * (scale * LOG2E)
        qx = qx_ref[:, tile(s)].astype(F32)
        shared = lane == 2 * ext
        qa = jnp.concatenate([jnp.where(lane < half, qm, 0.0),
                              jnp.where((lane < ext) | shared, qx, 0.0)], axis=1)
        qb = jnp.concatenate([jnp.where(lane >= half, qm, 0.0),
                              jnp.where(((lane >= ext) & (lane < 2 * ext)) | shared, qx, 0.0)], axis=1)
        for u in range(TQ // BLK):
            rows = slice(u * BLK, (u + 1) * BLK)
            qt_scr[s, :, u * CW:u * CW + BLK] = qa[rows].T.astype(BF16)
            qt_scr[s, :, u * CW + BLK:(u + 1) * CW] = qb[rows].T.astype(BF16)

    m_scr[...] = jnp.full(m_scr.shape, NEG, F32)
    l_scr[...] = jnp.zeros(l_scr.shape, F32)
    acc_scr[...] = jnp.zeros(acc_scr.shape, F32)

    def visible_keys(u, diagonal):
        return (u + 1) * BLK if diagonal else TQ

    def scores(c, s_ref, diagonal=False):
        off = pl.multiple_of(c * TQ, TQ)
        for s in range(NS):
            k = jnp.concatenate([km_ref[pl.ds(off, TQ), tile(s)], kx_ref[pl.ds(off, TQ), kx_tile(s)]],
                                axis=1)
            for u in range(TQ // BLK):
                nk = visible_keys(u, diagonal)
                s_ref[s, :nk, u * CW:(u + 1) * CW] = jnp.dot(k[:nk], qt_scr[s, :, u * CW:(u + 1) * CW],
                                                             preferred_element_type=F32)

    def update(c, s_ref, diagonal):
        for s in range(NS):
            vt = vt_scr[s, c]
            for u in range(TQ // BLK):
                cols = slice(u * CW, (u + 1) * CW)
                nk = visible_keys(u, diagonal)
                st = s_ref[s, :nk, cols]
                if diagonal:
                    key = lax.broadcasted_iota(jnp.int32, (nk, CW), 0)
                    query = u * BLK + (lax.broadcasted_iota(jnp.int32, (nk, CW), 1) & (BLK - 1))
                    st = jnp.where(key <= query, st, NEG)
                m_old = m_scr[s, :, cols]
                m_new = jnp.maximum(m_old, jnp.max(st, axis=0, keepdims=True))
                alpha = jnp.exp2(m_old - m_new)
                pt = jnp.exp2(st - m_new)
                l_scr[s, :, cols] = alpha * l_scr[s, :, cols] + jnp.sum(pt, axis=0, keepdims=True)
                pv = jnp.dot(vt[:, :nk], pt.astype(BF16), preferred_element_type=F32)
                acc_scr[s, :, cols] = alpha * acc_scr[s, :, cols] + pv
                m_scr[s, :, cols] = m_new

    scores(0, sa_scr)

    def pair(j, carry):
        c = 2 * j
        scores(c + 1, sb_scr)
        update(c, sa_scr, False)
        scores(c + 2, sa_scr)
        update(c + 1, sb_scr, False)
        return carry

    npairs = i // 2
    lax.fori_loop(0, npairs, pair, 0)
    odd_tail = i - 2 * npairs == 1

    @pl.when(odd_tail)
    def _():
        scores(i, sb_scr, diagonal=True)
        update(i - 1, sa_scr, False)
        update(i, sb_scr, True)

    @pl.when(jnp.logical_not(odd_tail))
    def _():
        update(i, sa_scr, True)

    feat = lax.broadcasted_iota(jnp.int32, (LANES, BLK), 0)
    for s in range(NS):
        out = acc_scr[s] / l_scr[s]
        for u in range(TQ // BLK):
            both = out[:, u * CW:(u + 1) * CW]
            o_ref[u * BLK:(u + 1) * BLK, tile(s)] = jnp.where(feat < half, both[:, :BLK],
                                                              both[:, BLK:]).T.astype(o_ref.dtype)


def _flash(qm, qm_tile, qx, qx_tile, km, km_tile, kx, kx_tile, kx_per_pair, v, v_tile,
           batch, seq, scale, ext):
    nq = seq // TQ
    groups = HEADS // 2 // NS
    w = NS * LANES
    assert qm_tile % NS == 0 and qx_tile % NS == 0 and km_tile % NS == 0 and v_tile % NS == 0
    if kx_per_pair:
        assert kx_tile % NS == 0
        kx_spec = pl.BlockSpec((seq, w), lambda b, p, i: (b, kx_tile // NS + p))
    else:
        kx_spec = pl.BlockSpec((seq, LANES), lambda b, p, i: (b, kx_tile))
    return pl.pallas_call(
        functools.partial(_flash_kernel, scale=scale, ext=ext, kx_per_pair=kx_per_pair),
        grid=(batch, groups, nq),
        in_specs=[pl.BlockSpec((TQ, w), lambda b, p, i: (b * nq + i, qm_tile // NS + p)),
                  pl.BlockSpec((TQ, w), lambda b, p, i: (b * nq + i, qx_tile // NS + p)),
                  pl.BlockSpec((seq, w), lambda b, p, i: (b, km_tile // NS + p)),
                  kx_spec,
                  pl.BlockSpec((seq, w), lambda b, p, i: (b, v_tile // NS + p))],
        out_specs=pl.BlockSpec((TQ, w), lambda b, p, i: (b * nq + i, p)),
        out_shape=jax.ShapeDtypeStruct((batch * seq, BRANCH_W), BF16),
        scratch_shapes=[pltpu.VMEM((NS, seq // TQ, LANES, TQ), BF16), pltpu.VMEM((NS, 2 * LANES, 2 * TQ), BF16),
                        pltpu.VMEM((NS, 1, 2 * TQ), F32), pltpu.VMEM((NS, 1, 2 * TQ), F32),
                        pltpu.VMEM((NS, LANES, 2 * TQ), F32),
                        pltpu.VMEM((NS, TQ, 2 * TQ), F32), pltpu.VMEM((NS, TQ, 2 * TQ), F32)],
        compiler_params=_cparams(3),
        name="flash",
    )(qm, qx, km, kx, v)


def _swa_kernel(sink_ref, q_ref, kp_ref, kc_ref, vp_ref, vc_ref, cosc_ref, sinc_ref, cosp_ref, sinp_ref,
                o_ref):
    j = pl.program_id(1)
    half = LANES // 2
    rh = HEAD_DIM // 2
    lane = lax.broadcasted_iota(jnp.int32, (BLK, LANES), 1)
    q = _rope(q_ref[...].astype(F32), cosc_ref[...], sinc_ref[...], rh) * (HEAD_DIM ** -0.5 * LOG2E)
    kc = _rope(kc_ref[...].astype(F32), cosc_ref[...], sinc_ref[...], rh)
    kp = _rope(kp_ref[...].astype(F32), cosp_ref[...], sinp_ref[...], rh)
    pairs = HEADS // 2
    group = lambda t: t // (pairs // SWA_KV_HEADS)

    def per_group(x):
        lane_k = lax.broadcasted_iota(jnp.int32, x.shape, 1)
        swapped = pltpu.roll(x, half, axis=1)
        return [jnp.where(lane_k < half, x, swapped), jnp.where(lane_k < half, swapped, x)]

    ks = [g.astype(BF16) for g in per_group(jnp.concatenate([kp, kc], axis=0))]
    v = jnp.concatenate([vp_ref[...], vc_ref[...]], axis=0).astype(F32)
    vts = [g.T.astype(BF16) for g in per_group(v)]

    units = [(r, t) for r in range(SWA_QB) for t in range(pairs)]
    sts = {}
    for r, t in units:
        qt = q[r * BLK:(r + 1) * BLK, t * LANES:(t + 1) * LANES]
        qtt = jnp.concatenate([jnp.where(lane < half, qt, 0.0).T, jnp.where(lane >= half, qt, 0.0).T],
                              axis=1).astype(BF16)
        keys = ks[group(t)][r * BLK:(r + 2) * BLK]
        sts[r, t] = jnp.dot(keys, qtt, preferred_element_type=F32)

    col1 = lax.broadcasted_iota(jnp.int32, (1, 2 * BLK), 1)
    feat = lax.broadcasted_iota(jnp.int32, (LANES, BLK), 0)
    kidx = lax.broadcasted_iota(jnp.int32, (2 * BLK, 2 * BLK), 0)
    col = lax.broadcasted_iota(jnp.int32, (2 * BLK, 2 * BLK), 1)
    d = kidx - (col & (BLK - 1))
    in_window = (d >= 1) & (d <= BLK)

    for r in range(SWA_QB):
        n = j * SWA_QB + r
        ok = in_window & (kidx >= PAD - (n - 1) * BLK)
        tiles = []
        for t in range(pairs):
            st = jnp.where(ok, sts[r, t], NEG)
            sink = jnp.where(col1 < BLK, sink_ref[2 * t], sink_ref[2 * t + 1]) * LOG2E
            m = jnp.maximum(jnp.max(st, axis=0, keepdims=True), sink)
            p = jnp.exp2(st - m)
            denom = jnp.sum(p, axis=0, keepdims=True) + jnp.exp2(sink - m)
            vt = vts[group(t)][:, r * BLK:(r + 2) * BLK]
            ot = jnp.dot(vt, p.astype(BF16), preferred_element_type=F32) / denom
            tiles.append(jnp.where(feat < half, ot[:, :BLK], ot[:, BLK:]).T)
        o_ref[r * BLK:(r + 1) * BLK, :] = jnp.concatenate(tiles, axis=1).astype(o_ref.dtype)


def _swa(proj, sinks, cos, sin, batch, seq):
    rows = SWA_QB * BLK
    nt = seq // rows
    kw = LANES
    prev = lambda j: jnp.maximum(j * SWA_QB - 1, 0)
    cur = lambda w, col: pl.BlockSpec((rows, w), lambda b, j: (b * nt + j, col))
    before = lambda w, col: pl.BlockSpec((BLK, w), lambda b, j: (b * nt * SWA_QB + prev(j), col))
    return pl.pallas_call(
        _swa_kernel,
        grid=(batch, nt),
        in_specs=[pl.BlockSpec(memory_space=pltpu.SMEM),
                  cur(BRANCH_W, (T_CQ * LANES) // BRANCH_W),
                  before(kw, (T_CK * LANES) // kw), cur(kw, (T_CK * LANES) // kw),
                  before(kw, (T_CV * LANES) // kw), cur(kw, (T_CV * LANES) // kw),
                  pl.BlockSpec((rows, LANES), lambda b, j: (j, 0)),
                  pl.BlockSpec((rows, LANES), lambda b, j: (j, 0)),
                  pl.BlockSpec((BLK, LANES), lambda b, j: (prev(j), 0)),
                  pl.BlockSpec((BLK, LANES), lambda b, j: (prev(j), 0))],
        out_specs=pl.BlockSpec((rows, BRANCH_W), lambda b, j: (b * nt + j, 0)),
        out_shape=jax.ShapeDtypeStruct((batch * seq, BRANCH_W), BF16),
        compiler_params=_cparams(2),
        name="swa",
    )(sinks, proj, proj, proj, proj, proj, cos, sin, cos, sin)


def _out_kernel(ya_ref, yb_ref, yc_ref, za_ref, zb_ref, zc_ref, g_ref, x_ref, head_ref, wb_ref, wo_ref, fg_ref,
                o_ref, *, first, final_norm, tiles_per_seq):
    merged = None
    for n, (y_ref, z_ref) in enumerate(((ya_ref, za_ref), (yb_ref, zb_ref), (yc_ref, zc_ref))):
        zh = z_ref[...].astype(F32)
        br = (y_ref[...].astype(F32) * (zh * (1.0 + jnp.tanh(zh)))).astype(BF16)
        pr = jnp.dot(br, wb_ref[n], preferred_element_type=F32)
        gate2 = 1.0 + jnp.tanh(g_ref[:, n * D_MODEL:(n + 1) * D_MODEL].astype(F32))
        merged = gate2 * pr if merged is None else merged + gate2 * pr
    if first:
        x = _first_layer_rows(x_ref, head_ref, pl.program_id(0) % tiles_per_seq)
    else:
        x = x_ref[...]
    h = x + jnp.dot(merged.astype(BF16), wo_ref[...], preferred_element_type=F32)
    if final_norm:
        h = h * lax.rsqrt(jnp.mean(h * h, axis=-1, keepdims=True) + EPS) * fg_ref[...]
    o_ref[...] = h


def _out(ya, yb, yc, proj, x2d, head, wb, wo, l, fg, batch, seq, first, final):
    assert not (first and final)
    m = batch * seq
    full = lambda a: pl.BlockSpec(a.shape, lambda *_: (0,) * a.ndim)
    zcol = lambda t: (t * LANES) // BRANCH_W
    if final:
        tiles = (seq - BLK) // TM_FINAL
        grid = (batch, tiles)
        row = lambda w, col: pl.BlockSpec((pl.Element(TM_FINAL), pl.Element(w)),
                                          lambda b, i: (pl.multiple_of(b * seq + BLK + i * TM_FINAL, BLK),
                                                        col * w))
        out_spec = pl.BlockSpec((TM_FINAL, D_MODEL), lambda b, i: (b * tiles + i, 0))
        out_rows = batch * (seq - BLK)
    else:
        grid = (m // TM_OUT,)
        row = lambda w, col: pl.BlockSpec((TM_OUT, w), lambda i: (i, col))
        out_spec = row(D_MODEL, 0)
        out_rows = m
    x_spec = _first_layer_spec(TM_OUT, seq, x2d.shape[0] // batch, lambda i: i) if first else row(D_MODEL, 0)
    return pl.pallas_call(
        functools.partial(_out_kernel, first=first, final_norm=final, tiles_per_seq=seq // TM_OUT),
        grid=grid,
        in_specs=[row(BRANCH_W, 0), row(BRANCH_W, 0), row(BRANCH_W, 0),
                  row(BRANCH_W, zcol(T_AZ)), row(BRANCH_W, zcol(T_BZ)), row(BRANCH_W, zcol(T_CZ)),
                  row(N_BRANCH * D_MODEL, 0), x_spec, full(head), _layer_block(wb, l), _layer_block(wo, l),
                  full(fg)],
        out_specs=out_spec,
        out_shape=jax.ShapeDtypeStruct((out_rows, D_MODEL), F32),
        compiler_params=_cparams(len(grid)),
        name="out",
    )(ya, yb, yc, proj, proj, proj, proj, x2d, head, wb, wo, fg)


def _w_in_pieces():
    sizes = (512, 512, 512, 8, 512, MLA_QLORA, MLA_KVLORA, MLA_ROPE, 512, 512, 128, 128, 512, 3072)
    offs = np.concatenate([[0], np.cumsum(sizes)])
    (a_q, a_k, a_v, a_f, a_z, b_cq, b_ckv, b_kr, b_z, c_q, c_k, c_v, c_z, gates) = [int(o) for o in offs[:-1]]
    dst = lambda t: t * LANES
    pieces = [(gates, 3072, dst(T_GATES), 0.5), (a_q, 512, dst(T_AQ), 1.0), (a_k, 512, dst(T_AK), 1.0),
              (a_v, 512, dst(T_AV), 1.0), (a_z, 512, dst(T_AZ), 0.5), (b_z, 512, dst(T_BZ), 0.5),
              (c_q, 512, dst(T_CQ), 1.0), (c_z, 512, dst(T_CZ), 0.5),
              (b_cq, MLA_QLORA, dst(T_MLA), 1.0), (b_ckv, MLA_KVLORA, dst(T_MLA) + MLA_QLORA, 1.0),
              (a_f, HEADS, dst(T_AF) + AF_LANE, 1.0),
              (c_k, SWA_KV_HEADS * HEAD_DIM, dst(T_CK), 1.0), (c_v, SWA_KV_HEADS * HEAD_DIM, dst(T_CV), 1.0)]
    assert dst(T_AF) == dst(T_MLA) + MLA_QLORA + MLA_KVLORA and AF_LANE >= 3 * MLA_ROPE
    for rep in range(3):
        pieces.append((b_kr, MLA_ROPE, dst(T_AF) + rep * MLA_ROPE, 1.0))
    return pieces, int(offs[-1])


def _relayout_kernel(w_ref, o_ref):
    o_ref[:, T_AF * LANES:(T_AF + 1) * LANES] = jnp.zeros((o_ref.shape[0], LANES), o_ref.dtype)
    for src, width, dst, scale in _w_in_pieces()[0]:
        piece = w_ref[:, src:src + width]
        if scale != 1.0:
            piece = piece * scale
        o_ref[:, dst:dst + width] = piece.astype(o_ref.dtype)


def _relayout_w_in(w):
    depth, d, n_in = w.shape
    assert n_in == _w_in_pieces()[1]
    rows = 128
    return pl.pallas_call(
        _relayout_kernel,
        grid=(depth, d // rows),
        in_specs=[pl.BlockSpec((None, rows, n_in), lambda l, r: (l, r, 0))],
        out_specs=pl.BlockSpec((None, rows, N_PROJ), lambda l, r: (l, r, 0)),
        out_shape=jax.ShapeDtypeStruct((depth, d, N_PROJ), BF16),
        compiler_params=_cparams(2),
        name="w_in_relayout",
    )(w)


def _relayout_mla_weights(w_uq, w_ukv):
    depth = w_uq.shape[0]
    uq = w_uq.reshape(depth, MLA_QLORA, HEADS, MLA_NOPE + MLA_ROPE)
    wqn = uq[..., :MLA_NOPE].reshape(depth, MLA_QLORA, HEADS * MLA_NOPE)
    rope = uq[..., MLA_NOPE:].reshape(depth, MLA_QLORA, HEADS // 2, 2 * MLA_ROPE)
    wqr = jnp.concatenate([rope, jnp.zeros_like(rope)], axis=-1).reshape(depth, MLA_QLORA, HEADS // 2 * LANES)
    ukv = w_ukv.reshape(depth, MLA_KVLORA, HEADS, MLA_NOPE + MLA_V)
    wk = ukv[..., :MLA_NOPE].reshape(depth, MLA_KVLORA, HEADS * MLA_NOPE)
    wv = ukv[..., MLA_NOPE:].reshape(depth, MLA_KVLORA, HEADS * MLA_V)
    return wqn.astype(BF16), wqr.astype(BF16), wk.astype(BF16), wv.astype(BF16)


def _rope_tables(pos, half):
    inv = ROPE_THETA ** (-jnp.arange(half, dtype=F32) / half)
    ang = pos.astype(F32)[:, None] * inv[None, :]
    cos = jnp.cos(ang)
    sin = jnp.sin(ang)
    reps = LANES // (2 * half)
    cos_full = jnp.tile(jnp.concatenate([cos, cos], axis=1), (1, reps))
    sin_signed = jnp.tile(jnp.concatenate([-sin, sin], axis=1), (1, reps))
    return cos_full, sin_signed


def kernel(x, meta_tokens, norm_g, w_in, b_f, g_cq, g_ckv, w_uq, w_ukv, sinks, w_branch, w_out, final_g):
    batch, seq_in, d = x.shape
    seq = PAD + N_META + seq_in
    depth = w_in.shape[0]
    assert d == D_MODEL and seq % TQ == 0 and seq % TM_PREP == 0
    assert seq % TM_IN == 0 and seq % TM_OUT == 0 and seq_in % TM_FINAL == 0

    head = jnp.concatenate([jnp.zeros((PAD, d), x.dtype), meta_tokens.astype(x.dtype)], axis=0)
    windowed = depth > 1
    if windowed:
        h = x.reshape(batch * seq_in, d)
    else:
        h = jnp.concatenate([jnp.broadcast_to(head[None], (batch, BLK, d)), x], axis=1).reshape(batch * seq, d)
    pos = jnp.arange(seq) - PAD
    cos16, sin16 = _rope_tables(pos, MLA_ROPE // 2)
    cos32, sin32 = _rope_tables(pos, HEAD_DIM // 2)
    fg = final_g.reshape(1, d).astype(F32)

    w = _relayout_w_in(w_in)
    wqn, wqr, wk, wv = _relayout_mla_weights(w_uq, w_ukv)
    wb = w_branch.astype(BF16)
    wo = (0.5 * w_out).astype(BF16)

    for l in range(depth):
        bf_row = jnp.zeros((1, LANES), F32).at[0, AF_LANE:AF_LANE + HEADS].set(b_f[l].astype(F32))

        first = windowed and l == 0
        proj, af = _inproj(h, head, norm_g[l].reshape(1, d).astype(F32), w, l, batch, seq, first)

        qx, kx = _fox_prep(af, bf_row, batch, seq)
        y_a = _flash(proj, T_AQ, qx, 0, proj, T_AK, kx, 0, True, proj, T_AV,
                     batch, seq, HEAD_DIM ** -0.5, FOX_EXT)

        qn, qr, kn, kr, vb = _mla_prep(proj, g_cq[l].reshape(1, -1).astype(F32),
                                       g_ckv[l].reshape(1, -1).astype(F32),
                                       wqn, wqr, wk, wv, l, cos16, sin16, seq // TM_PREP)
        y_b = _flash(qn, 0, qr, 0, kn, 0, kr, 0, False, vb, 0, batch, seq,
                     (MLA_NOPE + MLA_ROPE) ** -0.5, MLA_EXT)

        y_c = _swa(proj, sinks[l].astype(F32), cos32, sin32, batch, seq)

        h = _out(y_a, y_b, y_c, proj, h, head, wb, wo, l, fg, batch, seq, first, final=(l == depth - 1))

    return h.reshape(batch, seq_in, d)
```

```python
import functools

import numpy as np
import jax
import jax.numpy as jnp
from jax import lax
from jax.experimental import pallas as pl
from jax.experimental.pallas import tpu as pltpu

F32 = jnp.float32
BF16 = jnp.bfloat16

D_MODEL = 1024
N_META = 16
BLK = 128
PAD = BLK - N_META
ROPE_THETA = 10000.0
EPS = 1e-6
NEG = -1e30

HEADS = 8
HEAD_DIM = 64
MLA_NOPE = 64
MLA_ROPE = 32
MLA_V = 64
MLA_QLORA = 384
MLA_KVLORA = 256
SWA_KV_HEADS = 2
BRANCH_W = 512
N_BRANCH = 3

LANES = 128
N_PROJ = 7680
VMEM_LIMIT = 56 * 1024 * 1024

T_GATES = 0
T_AQ = 24
T_AK = 28
T_AV = 32
T_AZ = 36
T_BZ = 40
T_CQ = 44
T_CZ = 48
T_CK = 52
T_CV = 53
T_MLA = 54
T_AF = 59
AF_LANE = 96

TM_IN = 1408
TN_IN = 1536
TM_PREP = 1408
TM_OUT = 384
TM_FINAL = 512
TQ = 384
CW = 256
NS = 4
SWA_QB = 3
LOG2E = 1.4426950408889634
FOX_EXT = 6
FOX_STRIDE = 16
MLA_EXT = MLA_ROPE


def _cparams(n_axes):
    return pltpu.CompilerParams(dimension_semantics=("arbitrary",) * n_axes,
                                vmem_limit_bytes=VMEM_LIMIT)


def _first_layer_rows(x_ref, head_ref, tile_in_seq):
    xb = x_ref[...]
    shifted = jnp.concatenate([head_ref[...], xb[:xb.shape[0] - BLK]], axis=0)
    return jnp.where(tile_in_seq == 0, shifted, xb)


def _first_layer_spec(tm, seq, seq_in, tile_of):
    tiles = seq // tm

    def index_map(*idx):
        t = tile_of(*idx)
        start = (t // tiles) * seq_in + jnp.maximum((t % tiles) * tm - BLK, 0)
        return pl.multiple_of(start, BLK), 0

    return pl.BlockSpec((pl.Element(tm), pl.Element(D_MODEL)), index_map)


def _inproj_kernel(x_ref, head_ref, g_ref, w_ref, p_ref, af_ref, h_scr, *, first, tiles_per_seq):
    j = pl.program_id(1)

    @pl.when(j == 0)
    def _():
        if first:
            x = _first_layer_rows(x_ref, head_ref, pl.program_id(0) % tiles_per_seq)
        else:
            x = x_ref[...]
        ms = jnp.mean(x * x, axis=-1, keepdims=True)
        h_scr[...] = (x * lax.rsqrt(ms + EPS) * g_ref[...]).astype(BF16)

    acc = jnp.dot(h_scr[...], w_ref[...], preferred_element_type=F32)
    p_ref[...] = acc.astype(BF16)

    @pl.when(j == (T_AF * LANES) // TN_IN)
    def _():
        off = (T_AF * LANES) % TN_IN
        af_ref[...] = acc[:, off:off + LANES]


def _layer_block(a, l):
    return pl.BlockSpec((None,) + a.shape[1:], lambda *_: (l,) + (0,) * (a.ndim - 1))


def _inproj(x2d, head, g, w, l, batch, seq, first):
    m = batch * seq
    if first:
        x_spec = _first_layer_spec(TM_IN, seq, x2d.shape[0] // batch, lambda i, j: i)
    else:
        x_spec = pl.BlockSpec((TM_IN, D_MODEL), lambda i, j: (i, 0))
    return pl.pallas_call(
        functools.partial(_inproj_kernel, first=first, tiles_per_seq=seq // TM_IN),
        grid=(m // TM_IN, N_PROJ // TN_IN),
        in_specs=[x_spec,
                  pl.BlockSpec(head.shape, lambda i, j: (0, 0)),
                  pl.BlockSpec((1, D_MODEL), lambda i, j: (0, 0)),
                  pl.BlockSpec((None, D_MODEL, TN_IN), lambda i, j: (l, 0, j))],
        out_specs=[pl.BlockSpec((TM_IN, TN_IN), lambda i, j: (i, j)),
                   pl.BlockSpec((TM_IN, LANES), lambda i, j: (i, 0))],
        out_shape=[jax.ShapeDtypeStruct((m, N_PROJ), BF16),
                   jax.ShapeDtypeStruct((m, LANES), F32)],
        scratch_shapes=[pltpu.VMEM((TM_IN, D_MODEL), BF16)],
        compiler_params=_cparams(2),
        name="inproj",
    )(x2d, head, g, w)


def _rope(x, cos, sin_signed, half):
    width = x.shape[1]
    reps = width // LANES
    if reps > 1:
        cos = jnp.concatenate([cos] * reps, axis=1)
        sin_signed = jnp.concatenate([sin_signed] * reps, axis=1)
    lane = lax.broadcasted_iota(jnp.int32, x.shape, 1)
    up = pltpu.roll(x, width - half, axis=1)
    down = pltpu.roll(x, half, axis=1)
    swapped = jnp.where((lane & (2 * half - 1)) < half, up, down)
    return x * cos + swapped * sin_signed


def _mla_prep_kernel(p_ref, gq_ref, gkv_ref, wqn_ref, wqr_ref, wk_ref, wv_ref, cos_ref, sin_ref,
                     qn_ref, qr_ref, kn_ref, kr_ref, v_ref, *, seq_tiles):
    blk = p_ref[...].astype(F32)
    cq = blk[:, :MLA_QLORA]
    ckv = blk[:, MLA_QLORA:MLA_QLORA + MLA_KVLORA]
    kr = blk[:, MLA_QLORA + MLA_KVLORA:]
    cq = (cq * lax.rsqrt(jnp.mean(cq * cq, axis=-1, keepdims=True) + EPS) * gq_ref[...]).astype(BF16)
    ckv = (ckv * lax.rsqrt(jnp.mean(ckv * ckv, axis=-1, keepdims=True) + EPS) * gkv_ref[...]).astype(BF16)
    scale = (MLA_NOPE + MLA_ROPE) ** -0.5
    cos = cos_ref[...]
    sin = sin_ref[...]
    qn_ref[...] = jnp.dot(cq, wqn_ref[...], preferred_element_type=F32).astype(BF16)
    qr = jnp.dot(cq, wqr_ref[...], preferred_element_type=F32)
    qr = _rope(qr, cos, sin, MLA_ROPE // 2) * (scale * LOG2E)
    qlane = lax.broadcasted_iota(jnp.int32, qr.shape, 1) & (LANES - 1)
    qr_ref[...] = jnp.where(qlane == 2 * MLA_EXT, 1.0, qr).astype(BF16)
    kn_ref[...] = jnp.dot(ckv, wk_ref[...], preferred_element_type=F32).astype(BF16)
    v_ref[...] = jnp.dot(ckv, wv_ref[...], preferred_element_type=F32).astype(BF16)
    kr = _rope(kr, cos, sin, MLA_ROPE // 2)
    pos_in_seq = (pl.program_id(0) % seq_tiles) * kr.shape[0] + lax.broadcasted_iota(jnp.int32, kr.shape, 0)
    klane = lax.broadcasted_iota(jnp.int32, kr.shape, 1)
    kr_ref[...] = jnp.where(klane == 2 * MLA_EXT, jnp.where(pos_in_seq < PAD, NEG, 0.0), kr).astype(BF16)


def _mla_prep(proj, gq, gkv, wqn, wqr, wk, wv, l, cos, sin, seq_tiles):
    m = proj.shape[0]
    full = lambda a: pl.BlockSpec(a.shape, lambda i: (0,) * a.ndim)
    layer = lambda a: _layer_block(a, l)
    row = lambda w: pl.BlockSpec((TM_PREP, w), lambda i: (i, 0))
    tab = pl.BlockSpec((TM_PREP, LANES), lambda i: (i % seq_tiles, 0))
    mla_w = MLA_QLORA + MLA_KVLORA + LANES
    return pl.pallas_call(
        functools.partial(_mla_prep_kernel, seq_tiles=seq_tiles),
        grid=(m // TM_PREP,),
        in_specs=[pl.BlockSpec((TM_PREP, mla_w), lambda i: (i, (T_MLA * LANES) // mla_w)),
                  full(gq), full(gkv), layer(wqn), layer(wqr), layer(wk), layer(wv), tab, tab],
        out_specs=[row(BRANCH_W), row(BRANCH_W), row(BRANCH_W), row(LANES), row(BRANCH_W)],
        out_shape=[jax.ShapeDtypeStruct((m, BRANCH_W), BF16),
                   jax.ShapeDtypeStruct((m, BRANCH_W), BF16),
                   jax.ShapeDtypeStruct((m, BRANCH_W), BF16),
                   jax.ShapeDtypeStruct((m, LANES), BF16),
                   jax.ShapeDtypeStruct((m, BRANCH_W), BF16)],
        compiler_params=_cparams(1),
        name="mla_prep",
    )(proj, gq, gkv, wqn, wqr, wk, wv, cos, sin)


def _split3(x):
    hi = x.astype(BF16)
    r1 = x - hi.astype(F32)
    mid = r1.astype(BF16)
    lo = (r1 - mid.astype(F32)).astype(BF16)
    return jnp.concatenate([hi, mid, lo], axis=1)


def _fox_prep_kernel(af_ref, bf_ref, selq_ref, selk_ref, oneq_ref, onek_ref, qx_ref, kx_ref,
                     lf_scr, c_scr):
    seq = af_ref.shape[0]
    x = af_ref[...] + bf_ref[...]
    lf_scr[...] = -(jnp.maximum(-x, 0.0) + jnp.log1p(jnp.exp(-jnp.abs(x))))
    r = lax.broadcasted_iota(jnp.int32, (BLK, BLK), 0)
    c = lax.broadcasted_iota(jnp.int32, (BLK, BLK), 1)
    tri = (c <= r).astype(BF16)

    for t in range(seq // BLK):
        rows = slice(t * BLK, (t + 1) * BLK)
        cs = jnp.dot(tri, _split3(lf_scr[rows, :]), preferred_element_type=F32)
        c_scr[rows, :] = cs[:, :LANES] + cs[:, LANES:2 * LANES] + cs[:, 2 * LANES:]
    totals = [c_scr[t * BLK - 1:t * BLK, :] for t in range(1, seq // BLK)]
    carry = jnp.zeros((1, LANES), F32)
    for t in range(1, seq // BLK):
        carry = carry + totals[t - 1]
        c_scr[t * BLK:(t + 1) * BLK, :] = c_scr[t * BLK:(t + 1) * BLK, :] + carry
    parts = _split3(c_scr[...] * LOG2E)
    qx_ref[...] = (jnp.dot(parts, selq_ref[...], preferred_element_type=F32) + oneq_ref[...]).astype(BF16)
    kx_ref[...] = (jnp.dot(parts, selk_ref[...], preferred_element_type=F32) + onek_ref[...]).astype(BF16)
    row = lax.broadcasted_iota(jnp.int32, (BLK, kx_ref.shape[1]), 0)
    lane = lax.broadcasted_iota(jnp.int32, (BLK, kx_ref.shape[1]), 1)
    pad_bias = (row < PAD) & ((lane & (FOX_STRIDE - 1)) == 2 * FOX_EXT) & (lane < HEADS // 2 * FOX_STRIDE)
    kx_ref[:BLK, :] = jnp.where(pad_bias, NEG, kx_ref[:BLK, :].astype(F32)).astype(BF16)


def _fox_select_constants():
    selq = np.zeros((3 * LANES, LANES), np.float32)
    selk = np.zeros((3 * LANES, LANES), np.float32)
    oneq = np.zeros((1, LANES), np.float32)
    onek = np.zeros((1, LANES), np.float32)
    for p in range(HEADS // 2):
        for s in range(2):
            h = 2 * p + s
            base = p * FOX_STRIDE + s * FOX_EXT
            for part in range(3):
                selq[part * LANES + AF_LANE + h, base + part] = 1.0
                selk[part * LANES + AF_LANE + h, base + 3 + part] = -1.0
                oneq[0, base + 3 + part] = 1.0
                onek[0, base + part] = 1.0
        oneq[0, p * FOX_STRIDE + 2 * FOX_EXT] = 1.0
    return (jnp.asarray(selq, BF16), jnp.asarray(selk, BF16), jnp.asarray(oneq), jnp.asarray(onek))


def _fox_prep(af, bf_row, batch, seq):
    selq, selk, oneq, onek = _fox_select_constants()
    full = lambda a: pl.BlockSpec(a.shape, lambda b: (0,) * a.ndim)
    return pl.pallas_call(
        _fox_prep_kernel,
        grid=(batch,),
        in_specs=[pl.BlockSpec((seq, LANES), lambda b: (b, 0)),
                  full(bf_row), full(selq), full(selk), full(oneq), full(onek)],
        out_specs=[pl.BlockSpec((seq, LANES), lambda b: (b, 0)),
                   pl.BlockSpec((seq, LANES), lambda b: (b, 0))],
        out_shape=[jax.ShapeDtypeStruct((batch * seq, LANES), BF16),
                   jax.ShapeDtypeStruct((batch * seq, LANES), BF16)],
        scratch_shapes=[pltpu.VMEM((seq, LANES), F32), pltpu.VMEM((seq, LANES), F32)],
        compiler_params=_cparams(1),
        name="fox_prep",
    )(af, bf_row, selq, selk, oneq, onek)


def _flash_kernel(qm_ref, qx_ref, km_ref, kx_ref, v_ref, o_ref, vt_scr, qt_scr, m_scr, l_scr, acc_scr,
                  sa_scr, sb_scr, *, scale, ext, x_stride):
    i = pl.program_id(2)
    half = LANES // 2
    nq2 = 2 * TQ
    tile = lambda s: slice(s * LANES, (s + 1) * LANES)
    qx_tile = tile if x_stride == 0 else (lambda s: slice(0, LANES))

    @pl.when(i == 0)
    def _():
        for s in range(NS):
            for c in range(v_ref.shape[0] // TQ):
                vt_scr[s, c] = v_ref[c * TQ:(c + 1) * TQ, tile(s)].astype(F32).T.astype(BF16)

    lane = lax.broadcasted_iota(jnp.int32, (TQ, LANES), 1)
    for s in range(NS):
        qm = qm_ref[:, tile(s)].astype(F32) * (scale * LOG2E)
        qx = qx_ref[:, qx_tile(s)].astype(F32)
        x0 = s * x_stride
        shared = lane == x0 + 2 * ext
        qa = jnp.concatenate([jnp.where(lane < half, qm, 0.0),
                              jnp.where(((lane >= x0) & (lane < x0 + ext)) | shared, qx, 0.0)], axis=1)
        qb = jnp.concatenate([jnp.where(lane >= half, qm, 0.0),
                              jnp.where(((lane >= x0 + ext) & (lane < x0 + 2 * ext)) | shared, qx, 0.0)], axis=1)
        for u in range(TQ // BLK):
            rows = slice(u * BLK, (u + 1) * BLK)
            qt_scr[s, :, u * CW:u * CW + BLK] = qa[rows].T.astype(BF16)
            qt_scr[s, :, u * CW + BLK:(u + 1) * CW] = qb[rows].T.astype(BF16)

    m_scr[...] = jnp.full(m_scr.shape, NEG, F32)
    l_scr[...] = jnp.zeros(l_scr.shape, F32)
    acc_scr[...] = jnp.zeros(acc_scr.shape, F32)

    def visible_keys(u, diagonal):
        return (u + 1) * BLK if diagonal else TQ

    def scores(c, s_ref, diagonal=False):
        off = pl.multiple_of(c * TQ, TQ)
        for s in range(NS):
            k = jnp.concatenate([km_ref[pl.ds(off, TQ), tile(s)], kx_ref[pl.ds(off, TQ), :]], axis=1)
            for u in range(TQ // BLK):
                nk = visible_keys(u, diagonal)
                s_ref[s, :nk, u * CW:(u + 1) * CW] = jnp.dot(k[:nk], qt_scr[s, :, u * CW:(u + 1) * CW],
                                                             preferred_element_type=F32)

    def update(c, s_ref, diagonal):
        for s in range(NS):
            vt = vt_scr[s, c]
            for u in range(TQ // BLK):
                cols = slice(u * CW, (u + 1) * CW)
                nk = visible_keys(u, diagonal)
                st = s_ref[s, :nk, cols]
                if diagonal:
                    key = lax.broadcasted_iota(jnp.int32, (nk, CW), 0)
                    query = u * BLK + (lax.broadcasted_iota(jnp.int32, (nk, CW), 1) & (BLK - 1))
                    st = jnp.where(key <= query, st, NEG)
                m_old = m_scr[s, :, cols]
                m_new = jnp.maximum(m_old, jnp.max(st, axis=0, keepdims=True))
                alpha = jnp.exp2(m_old - m_new)
                pt = jnp.exp2(st - m_new)
                l_scr[s, :, cols] = alpha * l_scr[s, :, cols] + jnp.sum(pt, axis=0, keepdims=True)
                pv = jnp.dot(vt[:, :nk], pt.astype(BF16), preferred_element_type=F32)
                acc_scr[s, :, cols] = alpha * acc_scr[s, :, cols] + pv
                m_scr[s, :, cols] = m_new

    scores(0, sa_scr)

    def pair(j, carry):
        c = 2 * j
        scores(c + 1, sb_scr)
        update(c, sa_scr, False)
        scores(c + 2, sa_scr)
        update(c + 1, sb_scr, False)
        return carry

    npairs = i // 2
    lax.fori_loop(0, npairs, pair, 0)
    odd_tail = i - 2 * npairs == 1

    @pl.when(odd_tail)
    def _():
        scores(i, sb_scr, diagonal=True)
        update(i - 1, sa_scr, False)
        update(i, sb_scr, True)

    @pl.when(jnp.logical_not(odd_tail))
    def _():
        update(i, sa_scr, True)

    feat = lax.broadcasted_iota(jnp.int32, (LANES, BLK), 0)
    for s in range(NS):
        out = acc_scr[s] / l_scr[s]
        for u in range(TQ // BLK):
            both = out[:, u * CW:(u + 1) * CW]
            o_ref[u * BLK:(u + 1) * BLK, tile(s)] = jnp.where(feat < half, both[:, :BLK],
                                                              both[:, BLK:]).T.astype(o_ref.dtype)


def _flash(qm, qm_tile, qx, km, km_tile, kx, v, v_tile, batch, seq, scale, ext, x_stride):
    nq = seq // TQ
    groups = HEADS // 2 // NS
    w = NS * LANES
    assert qm_tile % NS == 0 and km_tile % NS == 0 and v_tile % NS == 0
    if x_stride:
        assert groups == 1
        qx_spec = pl.BlockSpec((TQ, LANES), lambda b, p, i: (b * nq + i, 0))
    else:
        qx_spec = pl.BlockSpec((TQ, w), lambda b, p, i: (b * nq + i, p))
    return pl.pallas_call(
        functools.partial(_flash_kernel, scale=scale, ext=ext, x_stride=x_stride),
        grid=(batch, groups, nq),
        in_specs=[pl.BlockSpec((TQ, w), lambda b, p, i: (b * nq + i, qm_tile // NS + p)),
                  qx_spec,
                  pl.BlockSpec((seq, w), lambda b, p, i: (b, km_tile // NS + p)),
                  pl.BlockSpec((seq, LANES), lambda b, p, i: (b, 0)),
                  pl.BlockSpec((seq, w), lambda b, p, i: (b, v_tile // NS + p))],
        out_specs=pl.BlockSpec((TQ, w), lambda b, p, i: (b * nq + i, p)),
        out_shape=jax.ShapeDtypeStruct((batch * seq, BRANCH_W), BF16),
        scratch_shapes=[pltpu.VMEM((NS, seq // TQ, LANES, TQ), BF16), pltpu.VMEM((NS, 2 * LANES, 2 * TQ), BF16),
                        pltpu.VMEM((NS, 1, 2 * TQ), F32), pltpu.VMEM((NS, 1, 2 * TQ), F32),
                        pltpu.VMEM((NS, LANES, 2 * TQ), F32),
                        pltpu.VMEM((NS, TQ, 2 * TQ), F32), pltpu.VMEM((NS, TQ, 2 * TQ), F32)],
        compiler_params=_cparams(3),
        name="flash",
    )(qm, qx, km, kx, v)


def _swa_kernel(sink_ref, q_ref, kp_ref, kc_ref, vp_ref, vc_ref, cosc_ref, sinc_ref, cosp_ref, sinp_ref,
                o_ref):
    j = pl.program_id(1)
    half = LANES // 2
    rh = HEAD_DIM // 2
    lane = lax.broadcasted_iota(jnp.int32, (BLK, LANES), 1)
    q = _rope(q_ref[...].astype(F32), cosc_ref[...], sinc_ref[...], rh) * (HEAD_DIM ** -0.5 * LOG2E)
    kc = _rope(kc_ref[...].astype(F32), cosc_ref[...], sinc_ref[...], rh)
    kp = _rope(kp_ref[...].astype(F32), cosp_ref[...], sinp_ref[...], rh)
    pairs = HEADS // 2
    group = lambda t: t // (pairs // SWA_KV_HEADS)

    def per_group(x):
        lane_k = lax.broadcasted_iota(jnp.int32, x.shape, 1)
        swapped = pltpu.roll(x, half, axis=1)
        return [jnp.where(lane_k < half, x, swapped), jnp.where(lane_k < half, swapped, x)]

    ks = [g.astype(BF16) for g in per_group(jnp.concatenate([kp, kc], axis=0))]
    v = jnp.concatenate([vp_ref[...], vc_ref[...]], axis=0).astype(F32)
    vts = [g.T.astype(BF16) for g in per_group(v)]

    units = [(r, t) for r in range(SWA_QB) for t in range(pairs)]
    sts = {}
    for r, t in units:
        qt = q[r * BLK:(r + 1) * BLK, t * LANES:(t + 1) * LANES]
        qtt = jnp.concatenate([jnp.where(lane < half, qt, 0.0).T, jnp.where(lane >= half, qt, 0.0).T],
                              axis=1).astype(BF16)
        keys = ks[group(t)][r * BLK:(r + 2) * BLK]
        sts[r, t] = jnp.dot(keys, qtt, preferred_element_type=F32)

    col1 = lax.broadcasted_iota(jnp.int32, (1, 2 * BLK), 1)
    feat = lax.broadcasted_iota(jnp.int32, (LANES, BLK), 0)
    kidx = lax.broadcasted_iota(jnp.int32, (2 * BLK, 2 * BLK), 0)
    col = lax.broadcasted_iota(jnp.int32, (2 * BLK, 2 * BLK), 1)
    d = kidx - (col & (BLK - 1))
    in_window = (d >= 1) & (d <= BLK)

    for r in range(SWA_QB):
        n = j * SWA_QB + r
        ok = in_window & (kidx >= PAD - (n - 1) * BLK)
        tiles = []
        for t in range(pairs):
            st = jnp.where(ok, sts[r, t], NEG)
            sink = jnp.where(col1 < BLK, sink_ref[2 * t], sink_ref[2 * t + 1]) * LOG2E
            m = jnp.maximum(jnp.max(st, axis=0, keepdims=True), sink)
            p = jnp.exp2(st - m)
            denom = jnp.sum(p, axis=0, keepdims=True) + jnp.exp2(sink - m)
            vt = vts[group(t)][:, r * BLK:(r + 2) * BLK]
            ot = jnp.dot(vt, p.astype(BF16), preferred_element_type=F32) / denom
            tiles.append(jnp.where(feat < half, ot[:, :BLK], ot[:, BLK:]).T)
        o_ref[r * BLK:(r + 1) * BLK, :] = jnp.concatenate(tiles, axis=1).astype(o_ref.dtype)


def _swa(proj, sinks, cos, sin, batch, seq):
    rows = SWA_QB * BLK
    nt = seq // rows
    kw = LANES
    prev = lambda j: jnp.maximum(j * SWA_QB - 1, 0)
    cur = lambda w, col: pl.BlockSpec((rows, w), lambda b, j: (b * nt + j, col))
    before = lambda w, col: pl.BlockSpec((BLK, w), lambda b, j: (b * nt * SWA_QB + prev(j), col))
    return pl.pallas_call(
        _swa_kernel,
        grid=(batch, nt),
        in_specs=[pl.BlockSpec(memory_space=pltpu.SMEM),
                  cur(BRANCH_W, (T_CQ * LANES) // BRANCH_W),
                  before(kw, (T_CK * LANES) // kw), cur(kw, (T_CK * LANES) // kw),
                  before(kw, (T_CV * LANES) // kw), cur(kw, (T_CV * LANES) // kw),
                  pl.BlockSpec((rows, LANES), lambda b, j: (j, 0)),
                  pl.BlockSpec((rows, LANES), lambda b, j: (j, 0)),
                  pl.BlockSpec((BLK, LANES), lambda b, j: (prev(j), 0)),
                  pl.BlockSpec((BLK, LANES), lambda b, j: (prev(j), 0))],
        out_specs=pl.BlockSpec((rows, BRANCH_W), lambda b, j: (b * nt + j, 0)),
        out_shape=jax.ShapeDtypeStruct((batch * seq, BRANCH_W), BF16),
        compiler_params=_cparams(2),
        name="swa",
    )(sinks, proj, proj, proj, proj, proj, cos, sin, cos, sin)


def _out_kernel(ya_ref, yb_ref, yc_ref, za_ref, zb_ref, zc_ref, g_ref, x_ref, head_ref, wb_ref, wo_ref, fg_ref,
                o_ref, *, first, final_norm, tiles_per_seq):
    merged = None
    for n, (y_ref, z_ref) in enumerate(((ya_ref, za_ref), (yb_ref, zb_ref), (yc_ref, zc_ref))):
        zh = z_ref[...].astype(F32)
        br = (y_ref[...].astype(F32) * (zh * (1.0 + jnp.tanh(zh)))).astype(BF16)
        pr = jnp.dot(br, wb_ref[n], preferred_element_type=F32)
        gate2 = 1.0 + jnp.tanh(g_ref[:, n * D_MODEL:(n + 1) * D_MODEL].astype(F32))
        merged = gate2 * pr if merged is None else merged + gate2 * pr
    if first:
        x = _first_layer_rows(x_ref, head_ref, pl.program_id(0) % tiles_per_seq)
    else:
        x = x_ref[...]
    h = x + jnp.dot(merged.astype(BF16), wo_ref[...], preferred_element_type=F32)
    if final_norm:
        h = h * lax.rsqrt(jnp.mean(h * h, axis=-1, keepdims=True) + EPS) * fg_ref[...]
    o_ref[...] = h


def _out(ya, yb, yc, proj, x2d, head, wb, wo, l, fg, batch, seq, first, final):
    assert not (first and final)
    m = batch * seq
    full = lambda a: pl.BlockSpec(a.shape, lambda *_: (0,) * a.ndim)
    zcol = lambda t: (t * LANES) // BRANCH_W
    if final:
        tiles = (seq - BLK) // TM_FINAL
        grid = (batch, tiles)
        row = lambda w, col: pl.BlockSpec((pl.Element(TM_FINAL), pl.Element(w)),
                                          lambda b, i: (pl.multiple_of(b * seq + BLK + i * TM_FINAL, BLK),
                                                        col * w))
        out_spec = pl.BlockSpec((TM_FINAL, D_MODEL), lambda b, i: (b * tiles + i, 0))
        out_rows = batch * (seq - BLK)
    else:
        grid = (m // TM_OUT,)
        row = lambda w, col: pl.BlockSpec((TM_OUT, w), lambda i: (i, col))
        out_spec = row(D_MODEL, 0)
        out_rows = m
    x_spec = _first_layer_spec(TM_OUT, seq, x2d.shape[0] // batch, lambda i: i) if first else row(D_MODEL, 0)
    return pl.pallas_call(
        functools.partial(_out_kernel, first=first, final_norm=final, tiles_per_seq=seq // TM_OUT),
        grid=grid,
        in_specs=[row(BRANCH_W, 0), row(BRANCH_W, 0), row(BRANCH_W, 0),
                  row(BRANCH_W, zcol(T_AZ)), row(BRANCH_W, zcol(T_BZ)), row(BRANCH_W, zcol(T_CZ)),
                  row(N_BRANCH * D_MODEL, 0), x_spec, full(head), _layer_block(wb, l), _layer_block(wo, l),
                  full(fg)],
        out_specs=out_spec,
        out_shape=jax.ShapeDtypeStruct((out_rows, D_MODEL), F32),
        compiler_params=_cparams(len(grid)),
        name="out",
    )(ya, yb, yc, proj, proj, proj, proj, x2d, head, wb, wo, fg)


def _w_in_pieces():
    sizes = (512, 512, 512, 8, 512, MLA_QLORA, MLA_KVLORA, MLA_ROPE, 512, 512, 128, 128, 512, 3072)
    offs = np.concatenate([[0], np.cumsum(sizes)])
    (a_q, a_k, a_v, a_f, a_z, b_cq, b_ckv, b_kr, b_z, c_q, c_k, c_v, c_z, gates) = [int(o) for o in offs[:-1]]
    dst = lambda t: t * LANES
    pieces = [(gates, 3072, dst(T_GATES), 0.5), (a_q, 512, dst(T_AQ), 1.0), (a_k, 512, dst(T_AK), 1.0),
              (a_v, 512, dst(T_AV), 1.0), (a_z, 512, dst(T_AZ), 0.5), (b_z, 512, dst(T_BZ), 0.5),
              (c_q, 512, dst(T_CQ), 1.0), (c_z, 512, dst(T_CZ), 0.5),
              (b_cq, MLA_QLORA, dst(T_MLA), 1.0), (b_ckv, MLA_KVLORA, dst(T_MLA) + MLA_QLORA, 1.0),
              (a_f, HEADS, dst(T_AF) + AF_LANE, 1.0),
              (c_k, SWA_KV_HEADS * HEAD_DIM, dst(T_CK), 1.0), (c_v, SWA_KV_HEADS * HEAD_DIM, dst(T_CV), 1.0)]
    assert dst(T_AF) == dst(T_MLA) + MLA_QLORA + MLA_KVLORA and AF_LANE >= 3 * MLA_ROPE
    for rep in range(3):
        pieces.append((b_kr, MLA_ROPE, dst(T_AF) + rep * MLA_ROPE, 1.0))
    return pieces, int(offs[-1])


def _relayout_kernel(w_ref, o_ref):
    o_ref[:, T_AF * LANES:(T_AF + 1) * LANES] = jnp.zeros((o_ref.shape[0], LANES), o_ref.dtype)
    for src, width, dst, scale in _w_in_pieces()[0]:
        piece = w_ref[:, src:src + width]
        if scale != 1.0:
            piece = piece * scale
        o_ref[:, dst:dst + width] = piece.astype(o_ref.dtype)


def _relayout_w_in(w):
    depth, d, n_in = w.shape
    assert n_in == _w_in_pieces()[1]
    rows = 128
    return pl.pallas_call(
        _relayout_kernel,
        grid=(depth, d // rows),
        in_specs=[pl.BlockSpec((None, rows, n_in), lambda l, r: (l, r, 0))],
        out_specs=pl.BlockSpec((None, rows, N_PROJ), lambda l, r: (l, r, 0)),
        out_shape=jax.ShapeDtypeStruct((depth, d, N_PROJ), BF16),
        compiler_params=_cparams(2),
        name="w_in_relayout",
    )(w)


def _relayout_mla_weights(w_uq, w_ukv):
    depth = w_uq.shape[0]
    uq = w_uq.reshape(depth, MLA_QLORA, HEADS, MLA_NOPE + MLA_ROPE)
    wqn = uq[..., :MLA_NOPE].reshape(depth, MLA_QLORA, HEADS * MLA_NOPE)
    rope = uq[..., MLA_NOPE:].reshape(depth, MLA_QLORA, HEADS // 2, 2 * MLA_ROPE)
    wqr = jnp.concatenate([rope, jnp.zeros_like(rope)], axis=-1).reshape(depth, MLA_QLORA, HEADS // 2 * LANES)
    ukv = w_ukv.reshape(depth, MLA_KVLORA, HEADS, MLA_NOPE + MLA_V)
    wk = ukv[..., :MLA_NOPE].reshape(depth, MLA_KVLORA, HEADS * MLA_NOPE)
    wv = ukv[..., MLA_NOPE:].reshape(depth, MLA_KVLORA, HEADS * MLA_V)
    return wqn.astype(BF16), wqr.astype(BF16), wk.astype(BF16), wv.astype(BF16)


def _rope_tables(pos, half):
    inv = ROPE_THETA ** (-jnp.arange(half, dtype=F32) / half)
    ang = pos.astype(F32)[:, None] * inv[None, :]
    cos = jnp.cos(ang)
    sin = jnp.sin(ang)
    reps = LANES // (2 * half)
    cos_full = jnp.tile(jnp.concatenate([cos, cos], axis=1), (1, reps))
    sin_signed = jnp.tile(jnp.concatenate([-sin, sin], axis=1), (1, reps))
    return cos_full, sin_signed


def kernel(x, meta_tokens, norm_g, w_in, b_f, g_cq, g_ckv, w_uq, w_ukv, sinks, w_branch, w_out, final_g):
    batch, seq_in, d = x.shape
    seq = PAD + N_META + seq_in
    depth = w_in.shape[0]
    assert d == D_MODEL and seq % TQ == 0 and seq % TM_PREP == 0
    assert seq % TM_IN == 0 and seq % TM_OUT == 0 and seq_in % TM_FINAL == 0

    head = jnp.concatenate([jnp.zeros((PAD, d), x.dtype), meta_tokens.astype(x.dtype)], axis=0)
    windowed = depth > 1
    if windowed:
        h = x.reshape(batch * seq_in, d)
    else:
        h = jnp.concatenate([jnp.broadcast_to(head[None], (batch, BLK, d)), x], axis=1).reshape(batch * seq, d)
    pos = jnp.arange(seq) - PAD
    cos16, sin16 = _rope_tables(pos, MLA_ROPE // 2)
    cos32, sin32 = _rope_tables(pos, HEAD_DIM // 2)
    fg = final_g.reshape(1, d).astype(F32)

    w = _relayout_w_in(w_in)
    wqn, wqr, wk, wv = _relayout_mla_weights(w_uq, w_ukv)
    wb = w_branch.astype(BF16)
    wo = (0.5 * w_out).astype(BF16)

    for l in range(depth):
        bf_row = jnp.zeros((1, LANES), F32).at[0, AF_LANE:AF_LANE + HEADS].set(b_f[l].astype(F32))

        first = windowed and l == 0
        proj, af = _inproj(h, head, norm_g[l].reshape(1, d).astype(F32), w, l, batch, seq, first)

        qx, kx = _fox_prep(af, bf_row, batch, seq)
        y_a = _flash(proj, T_AQ, qx, proj, T_AK, kx, proj, T_AV, batch, seq, HEAD_DIM ** -0.5, FOX_EXT, FOX_STRIDE)

        qn, qr, kn, kr, vb = _mla_prep(proj, g_cq[l].reshape(1, -1).astype(F32),
                                       g_ckv[l].reshape(1, -1).astype(F32),
                                       wqn, wqr, wk, wv, l, cos16, sin16, seq // TM_PREP)
        y_b = _flash(qn, 0, qr, kn, 0, kr, vb, 0, batch, seq, (MLA_NOPE + MLA_ROPE) ** -0.5, MLA_EXT, 0)

        y_c = _swa(proj, sinks[l].astype(F32), cos32, sin32, batch, seq)

        h = _out(y_a, y_b, y_c, proj, h, head, wb, wo, l, fg, batch, seq, first, final=(l == depth - 1))

    return h.reshape(batch, seq_in, d)
```

```python
import functools

import numpy as np
import jax
import jax.numpy as jnp
from jax import lax
from jax.experimental import pallas as pl
from jax.experimental.pallas import tpu as pltpu

F32 = jnp.float32
BF16 = jnp.bfloat16

D_MODEL = 1024
N_META = 16
BLK = 128
PAD = BLK - N_META
ROPE_THETA = 10000.0
EPS = 1e-6
NEG = -1e30

HEADS = 8
HEAD_DIM = 64
MLA_NOPE = 64
MLA_ROPE = 32
MLA_V = 64
MLA_QLORA = 384
MLA_KVLORA = 256
SWA_KV_HEADS = 2
BRANCH_W = 512
N_BRANCH = 3

LANES = 128
N_PROJ = 7680
VMEM_LIMIT = 56 * 1024 * 1024

T_GATES = 0
T_AQ = 24
T_AK = 28
T_AV = 32
T_AZ = 36
T_BZ = 40
T_CQ = 44
T_CZ = 48
T_CK = 52
T_CV = 53
T_MLA = 54
T_AF = 59
AF_LANE = 96

TM_IN = 1408
TN_IN = 2560
TM_PREP = 1408
TM_OUT = 384
TM_FINAL = 512
TQ = 384
CW = 256
NS = 4
SWA_QB = 3
LOG2E = 1.4426950408889634
FOX_EXT = 6
FOX_STRIDE = 16
MLA_EXT = MLA_ROPE


def _cparams(n_axes):
    return pltpu.CompilerParams(dimension_semantics=("arbitrary",) * n_axes,
                                vmem_limit_bytes=VMEM_LIMIT)


def _first_layer_rows(x_ref, head_ref, tile_in_seq):
    xb = x_ref[...]
    shifted = jnp.concatenate([head_ref[...], xb[:xb.shape[0] - BLK]], axis=0)
    return jnp.where(tile_in_seq == 0, shifted, xb)


def _first_layer_spec(tm, seq, seq_in, tile_of):
    tiles = seq // tm

    def index_map(*idx):
        t = tile_of(*idx)
        start = (t // tiles) * seq_in + jnp.maximum((t % tiles) * tm - BLK, 0)
        return pl.multiple_of(start, BLK), 0

    return pl.BlockSpec((pl.Element(tm), pl.Element(D_MODEL)), index_map)


def _inproj_kernel(x_ref, head_ref, g_ref, w_ref, p_ref, af_ref, h_scr, *, first, tiles_per_seq):
    j = pl.program_id(1)

    @pl.when(j == 0)
    def _():
        if first:
            x = _first_layer_rows(x_ref, head_ref, pl.program_id(0) % tiles_per_seq)
        else:
            x = x_ref[...]
        ms = jnp.mean(x * x, axis=-1, keepdims=True)
        h_scr[...] = (x * lax.rsqrt(ms + EPS) * g_ref[...]).astype(BF16)

    acc = jnp.dot(h_scr[...], w_ref[...], preferred_element_type=F32)
    p_ref[...] = acc.astype(BF16)

    @pl.when(j == (T_AF * LANES) // TN_IN)
    def _():
        off = (T_AF * LANES) % TN_IN
        af_ref[...] = acc[:, off:off + LANES]


def _layer_block(a, l):
    return pl.BlockSpec((None,) + a.shape[1:], lambda *_: (l,) + (0,) * (a.ndim - 1))


def _inproj(x2d, head, g, w, l, batch, seq, first):
    m = batch * seq
    if first:
        x_spec = _first_layer_spec(TM_IN, seq, x2d.shape[0] // batch, lambda i, j: i)
    else:
        x_spec = pl.BlockSpec((TM_IN, D_MODEL), lambda i, j: (i, 0))
    return pl.pallas_call(
        functools.partial(_inproj_kernel, first=first, tiles_per_seq=seq // TM_IN),
        grid=(m // TM_IN, N_PROJ // TN_IN),
        in_specs=[x_spec,
                  pl.BlockSpec(head.shape, lambda i, j: (0, 0)),
                  pl.BlockSpec((1, D_MODEL), lambda i, j: (0, 0)),
                  pl.BlockSpec((None, D_MODEL, TN_IN), lambda i, j: (l, 0, j))],
        out_specs=[pl.BlockSpec((TM_IN, TN_IN), lambda i, j: (i, j)),
                   pl.BlockSpec((TM_IN, LANES), lambda i, j: (i, 0))],
        out_shape=[jax.ShapeDtypeStruct((m, N_PROJ), BF16),
                   jax.ShapeDtypeStruct((m, LANES), F32)],
        scratch_shapes=[pltpu.VMEM((TM_IN, D_MODEL), BF16)],
        compiler_params=_cparams(2),
        name="inproj",
    )(x2d, head, g, w)


def _rope(x, cos, sin_signed, half):
    width = x.shape[1]
    reps = width // LANES
    if reps > 1:
        cos = jnp.concatenate([cos] * reps, axis=1)
        sin_signed = jnp.concatenate([sin_signed] * reps, axis=1)
    lane = lax.broadcasted_iota(jnp.int32, x.shape, 1)
    up = pltpu.roll(x, width - half, axis=1)
    down = pltpu.roll(x, half, axis=1)
    swapped = jnp.where((lane & (2 * half - 1)) < half, up, down)
    return x * cos + swapped * sin_signed


def _mla_prep_kernel(p_ref, gq_ref, gkv_ref, wqn_ref, wqr_ref, wk_ref, wv_ref, cos_ref, sin_ref,
                     qn_ref, qr_ref, kn_ref, kr_ref, v_ref, *, seq_tiles):
    blk = p_ref[...].astype(F32)
    cq = blk[:, :MLA_QLORA]
    ckv = blk[:, MLA_QLORA:MLA_QLORA + MLA_KVLORA]
    kr = blk[:, MLA_QLORA + MLA_KVLORA:]
    cq = (cq * lax.rsqrt(jnp.mean(cq * cq, axis=-1, keepdims=True) + EPS) * gq_ref[...]).astype(BF16)
    ckv = (ckv * lax.rsqrt(jnp.mean(ckv * ckv, axis=-1, keepdims=True) + EPS) * gkv_ref[...]).astype(BF16)
    scale = (MLA_NOPE + MLA_ROPE) ** -0.5
    cos = cos_ref[...]
    sin = sin_ref[...]
    qn_ref[...] = jnp.dot(cq, wqn_ref[...], preferred_element_type=F32).astype(BF16)
    qr = jnp.dot(cq, wqr_ref[...], preferred_element_type=F32)
    qr = _rope(qr, cos, sin, MLA_ROPE // 2) * (scale * LOG2E)
    qlane = lax.broadcasted_iota(jnp.int32, qr.shape, 1) & (LANES - 1)
    qr_ref[...] = jnp.where(qlane == 2 * MLA_EXT, 1.0, qr).astype(BF16)
    kn_ref[...] = jnp.dot(ckv, wk_ref[...], preferred_element_type=F32).astype(BF16)
    v_ref[...] = jnp.dot(ckv, wv_ref[...], preferred_element_type=F32).astype(BF16)
    kr = _rope(kr, cos, sin, MLA_ROPE // 2)
    pos_in_seq = (pl.program_id(0) % seq_tiles) * kr.shape[0] + lax.broadcasted_iota(jnp.int32, kr.shape, 0)
    klane = lax.broadcasted_iota(jnp.int32, kr.shape, 1)
    kr_ref[...] = jnp.where(klane == 2 * MLA_EXT, jnp.where(pos_in_seq < PAD, NEG, 0.0), kr).astype(BF16)


def _mla_prep(proj, gq, gkv, wqn, wqr, wk, wv, l, cos, sin, seq_tiles):
    m = proj.shape[0]
    full = lambda a: pl.BlockSpec(a.shape, lambda i: (0,) * a.ndim)
    layer = lambda a: _layer_block(a, l)
    row = lambda w: pl.BlockSpec((TM_PREP, w), lambda i: (i, 0))
    tab = pl.BlockSpec((TM_PREP, LANES), lambda i: (i % seq_tiles, 0))
    mla_w = MLA_QLORA + MLA_KVLORA + LANES
    return pl.pallas_call(
        functools.partial(_mla_prep_kernel, seq_tiles=seq_tiles),
        grid=(m // TM_PREP,),
        in_specs=[pl.BlockSpec((TM_PREP, mla_w), lambda i: (i, (T_MLA * LANES) // mla_w)),
                  full(gq), full(gkv), layer(wqn), layer(wqr), layer(wk), layer(wv), tab, tab],
        out_specs=[row(BRANCH_W), row(BRANCH_W), row(BRANCH_W), row(LANES), row(BRANCH_W)],
        out_shape=[jax.ShapeDtypeStruct((m, BRANCH_W), BF16),
                   jax.ShapeDtypeStruct((m, BRANCH_W), BF16),
                   jax.ShapeDtypeStruct((m, BRANCH_W), BF16),
                   jax.ShapeDtypeStruct((m, LANES), BF16),
                   jax.ShapeDtypeStruct((m, BRANCH_W), BF16)],
        compiler_params=_cparams(1),
        name="mla_prep",
    )(proj, gq, gkv, wqn, wqr, wk, wv, cos, sin)


def _split3(x):
    hi = x.astype(BF16)
    r1 = x - hi.astype(F32)
    mid = r1.astype(BF16)
    lo = (r1 - mid.astype(F32)).astype(BF16)
    return jnp.concatenate([hi, mid, lo], axis=1)


def _fox_prep_kernel(af_ref, bf_ref, selq_ref, selk_ref, oneq_ref, onek_ref, qx_ref, kx_ref,
                     lf_scr, c_scr):
    seq = af_ref.shape[0]
    x = af_ref[...] + bf_ref[...]
    lf_scr[...] = -(jnp.maximum(-x, 0.0) + jnp.log1p(jnp.exp(-jnp.abs(x))))
    r = lax.broadcasted_iota(jnp.int32, (BLK, BLK), 0)
    c = lax.broadcasted_iota(jnp.int32, (BLK, BLK), 1)
    tri = (c <= r).astype(BF16)

    for t in range(seq // BLK):
        rows = slice(t * BLK, (t + 1) * BLK)
        cs = jnp.dot(tri, _split3(lf_scr[rows, :]), preferred_element_type=F32)
        c_scr[rows, :] = cs[:, :LANES] + cs[:, LANES:2 * LANES] + cs[:, 2 * LANES:]
    totals = [c_scr[t * BLK - 1:t * BLK, :] for t in range(1, seq // BLK)]
    carry = jnp.zeros((1, LANES), F32)
    for t in range(1, seq // BLK):
        carry = carry + totals[t - 1]
        c_scr[t * BLK:(t + 1) * BLK, :] = c_scr[t * BLK:(t + 1) * BLK, :] + carry
    parts = _split3(c_scr[...] * LOG2E)
    qx_ref[...] = (jnp.dot(parts, selq_ref[...], preferred_element_type=F32) + oneq_ref[...]).astype(BF16)
    kx_ref[...] = (jnp.dot(parts, selk_ref[...], preferred_element_type=F32) + onek_ref[...]).astype(BF16)
    row = lax.broadcasted_iota(jnp.int32, (BLK, kx_ref.shape[1]), 0)
    lane = lax.broadcasted_iota(jnp.int32, (BLK, kx_ref.shape[1]), 1)
    pad_bias = (row < PAD) & ((lane & (FOX_STRIDE - 1)) == 2 * FOX_EXT) & (lane < HEADS // 2 * FOX_STRIDE)
    kx_ref[:BLK, :] = jnp.where(pad_bias, NEG, kx_ref[:BLK, :].astype(F32)).astype(BF16)


def _fox_select_constants():
    selq = np.zeros((3 * LANES, LANES), np.float32)
    selk = np.zeros((3 * LANES, LANES), np.float32)
    oneq = np.zeros((1, LANES), np.float32)
    onek = np.zeros((1, LANES), np.float32)
    for p in range(HEADS // 2):
        for s in range(2):
            h = 2 * p + s
            base = p * FOX_STRIDE + s * FOX_EXT
            for part in range(3):
                selq[part * LANES + AF_LANE + h, base + part] = 1.0
                selk[part * LANES + AF_LANE + h, base + 3 + part] = -1.0
                oneq[0, base + 3 + part] = 1.0
                onek[0, base + part] = 1.0
        oneq[0, p * FOX_STRIDE + 2 * FOX_EXT] = 1.0
    return (jnp.asarray(selq, BF16), jnp.asarray(selk, BF16), jnp.asarray(oneq), jnp.asarray(onek))


def _fox_prep(af, bf_row, batch, seq):
    selq, selk, oneq, onek = _fox_select_constants()
    full = lambda a: pl.BlockSpec(a.shape, lambda b: (0,) * a.ndim)
    return pl.pallas_call(
        _fox_prep_kernel,
        grid=(batch,),
        in_specs=[pl.BlockSpec((seq, LANES), lambda b: (b, 0)),
                  full(bf_row), full(selq), full(selk), full(oneq), full(onek)],
        out_specs=[pl.BlockSpec((seq, LANES), lambda b: (b, 0)),
                   pl.BlockSpec((seq, LANES), lambda b: (b, 0))],
        out_shape=[jax.ShapeDtypeStruct((batch * seq, LANES), BF16),
                   jax.ShapeDtypeStruct((batch * seq, LANES), BF16)],
        scratch_shapes=[pltpu.VMEM((seq, LANES), F32), pltpu.VMEM((seq, LANES), F32)],
        compiler_params=_cparams(1),
        name="fox_prep",
    )(af, bf_row, selq, selk, oneq, onek)


def _flash_kernel(qm_ref, qx_ref, km_ref, kx_ref, v_ref, o_ref, vt_scr, qt_scr, m_scr, l_scr, acc_scr,
                  sa_scr, sb_scr, *, scale, ext, x_stride):
    i = pl.program_id(2)
    half = LANES // 2
    nq2 = 2 * TQ
    tile = lambda s: slice(s * LANES, (s + 1) * LANES)
    qx_tile = tile if x_stride == 0 else (lambda s: slice(0, LANES))

    @pl.when(i == 0)
    def _():
        for s in range(NS):
            for c in range(v_ref.shape[0] // TQ):
                vt_scr[s, c] = v_ref[c * TQ:(c + 1) * TQ, tile(s)].astype(F32).T.astype(BF16)

    lane = lax.broadcasted_iota(jnp.int32, (TQ, LANES), 1)
    for s in range(NS):
        qm = qm_ref[:, tile(s)].astype(F32) * (scale * LOG2E)
        qx = qx_ref[:, qx_tile(s)].astype(F32)
        x0 = s * x_stride
        shared = lane == x0 + 2 * ext
        qa = jnp.concatenate([jnp.where(lane < half, qm, 0.0),
                              jnp.where(((lane >= x0) & (lane < x0 + ext)) | shared, qx, 0.0)], axis=1)
        qb = jnp.concatenate([jnp.where(lane >= half, qm, 0.0),
                              jnp.where(((lane >= x0 + ext) & (lane < x0 + 2 * ext)) | shared, qx, 0.0)], axis=1)
        for u in range(TQ // BLK):
            rows = slice(u * BLK, (u + 1) * BLK)
            qt_scr[s, :, u * CW:u * CW + BLK] = qa[rows].T.astype(BF16)
            qt_scr[s, :, u * CW + BLK:(u + 1) * CW] = qb[rows].T.astype(BF16)

    m_scr[...] = jnp.full(m_scr.shape, NEG, F32)
    l_scr[...] = jnp.zeros(l_scr.shape, F32)
    acc_scr[...] = jnp.zeros(acc_scr.shape, F32)

    def visible_keys(u, diagonal):
        return (u + 1) * BLK if diagonal else TQ

    def scores(c, s_ref, diagonal=False):
        off = pl.multiple_of(c * TQ, TQ)
        for s in range(NS):
            k = jnp.concatenate([km_ref[pl.ds(off, TQ), tile(s)], kx_ref[pl.ds(off, TQ), :]], axis=1)
            for u in range(TQ // BLK):
                nk = visible_keys(u, diagonal)
                s_ref[s, :nk, u * CW:(u + 1) * CW] = jnp.dot(k[:nk], qt_scr[s, :, u * CW:(u + 1) * CW],
                                                             preferred_element_type=F32)

    def update(c, s_ref, diagonal):
        for s in range(NS):
            vt = vt_scr[s, c]
            for u in range(TQ // BLK):
                cols = slice(u * CW, (u + 1) * CW)
                nk = visible_keys(u, diagonal)
                st = s_ref[s, :nk, cols]
                if diagonal:
                    key = lax.broadcasted_iota(jnp.int32, (nk, CW), 0)
                    query = u * BLK + (lax.broadcasted_iota(jnp.int32, (nk, CW), 1) & (BLK - 1))
                    st = jnp.where(key <= query, st, NEG)
                m_old = m_scr[s, :, cols]
                m_new = jnp.maximum(m_old, jnp.max(st, axis=0, keepdims=True))
                alpha = jnp.exp2(m_old - m_new)
                pt = jnp.exp2(st - m_new)
                l_scr[s, :, cols] = alpha * l_scr[s, :, cols] + jnp.sum(pt, axis=0, keepdims=True)
                pv = jnp.dot(vt[:, :nk], pt.astype(BF16), preferred_element_type=F32)
                acc_scr[s, :, cols] = alpha * acc_scr[s, :, cols] + pv
                m_scr[s, :, cols] = m_new

    scores(0, sa_scr)

    def pair(j, carry):
        c = 2 * j
        scores(c + 1, sb_scr)
        update(c, sa_scr, False)
        scores(c + 2, sa_scr)
        update(c + 1, sb_scr, False)
        return carry

    npairs = i // 2
    lax.fori_loop(0, npairs, pair, 0)
    odd_tail = i - 2 * npairs == 1

    @pl.when(odd_tail)
    def _():
        scores(i, sb_scr, diagonal=True)
        update(i - 1, sa_scr, False)
        update(i, sb_scr, True)

    @pl.when(jnp.logical_not(odd_tail))
    def _():
        update(i, sa_scr, True)

    feat = lax.broadcasted_iota(jnp.int32, (LANES, BLK), 0)
    for s in range(NS):
        out = acc_scr[s] / l_scr[s]
        for u in range(TQ // BLK):
            both = out[:, u * CW:(u + 1) * CW]
            o_ref[u * BLK:(u + 1) * BLK, tile(s)] = jnp.where(feat < half, both[:, :BLK],
                                                              both[:, BLK:]).T.astype(o_ref.dtype)


def _flash(qm, qm_tile, qx, km, km_tile, kx, v, v_tile, batch, seq, scale, ext, x_stride):
    nq = seq // TQ
    groups = HEADS // 2 // NS
    w = NS * LANES
    assert qm_tile % NS == 0 and km_tile % NS == 0 and v_tile % NS == 0
    if x_stride:
        assert groups == 1
        qx_spec = pl.BlockSpec((TQ, LANES), lambda b, p, i: (b * nq + i, 0))
    else:
        qx_spec = pl.BlockSpec((TQ, w), lambda b, p, i: (b * nq + i, p))
    return pl.pallas_call(
        functools.partial(_flash_kernel, scale=scale, ext=ext, x_stride=x_stride),
        grid=(batch, groups, nq),
        in_specs=[pl.BlockSpec((TQ, w), lambda b, p, i: (b * nq + i, qm_tile // NS + p)),
                  qx_spec,
                  pl.BlockSpec((seq, w), lambda b, p, i: (b, km_tile // NS + p)),
                  pl.BlockSpec((seq, LANES), lambda b, p, i: (b, 0)),
                  pl.BlockSpec((seq, w), lambda b, p, i: (b, v_tile // NS + p))],
        out_specs=pl.BlockSpec((TQ, w), lambda b, p, i: (b * nq + i, p)),
        out_shape=jax.ShapeDtypeStruct((batch * seq, BRANCH_W), BF16),
        scratch_shapes=[pltpu.VMEM((NS, seq // TQ, LANES, TQ), BF16), pltpu.VMEM((NS, 2 * LANES, 2 * TQ), BF16),
                        pltpu.VMEM((NS, 1, 2 * TQ), F32), pltpu.VMEM((NS, 1, 2 * TQ), F32),
                        pltpu.VMEM((NS, LANES, 2 * TQ), F32),
                        pltpu.VMEM((NS, TQ, 2 * TQ), F32), pltpu.VMEM((NS, TQ, 2 * TQ), F32)],
        compiler_params=_cparams(3),
        name="flash",
    )(qm, qx, km, kx, v)


def _swa_kernel(sink_ref, q_ref, kp_ref, kc_ref, vp_ref, vc_ref, cosc_ref, sinc_ref, cosp_ref, sinp_ref,
                o_ref):
    j = pl.program_id(1)
    half = LANES // 2
    rh = HEAD_DIM // 2
    lane = lax.broadcasted_iota(jnp.int32, (BLK, LANES), 1)
    q = _rope(q_ref[...].astype(F32), cosc_ref[...], sinc_ref[...], rh) * (HEAD_DIM ** -0.5 * LOG2E)
    kc = _rope(kc_ref[...].astype(F32), cosc_ref[...], sinc_ref[...], rh)
    kp = _rope(kp_ref[...].astype(F32), cosp_ref[...], sinp_ref[...], rh)
    pairs = HEADS // 2
    group = lambda t: t // (pairs // SWA_KV_HEADS)

    def per_group(x):
        lane_k = lax.broadcasted_iota(jnp.int32, x.shape, 1)
        swapped = pltpu.roll(x, half, axis=1)
        return [jnp.where(lane_k < half, x, swapped), jnp.where(lane_k < half, swapped, x)]

    ks = [g.astype(BF16) for g in per_group(jnp.concatenate([kp, kc], axis=0))]
    v = jnp.concatenate([vp_ref[...], vc_ref[...]], axis=0).astype(F32)
    vts = [g.T.astype(BF16) for g in per_group(v)]

    units = [(r, t) for r in range(SWA_QB) for t in range(pairs)]
    sts = {}
    for r, t in units:
        qt = q[r * BLK:(r + 1) * BLK, t * LANES:(t + 1) * LANES]
        qtt = jnp.concatenate([jnp.where(lane < half, qt, 0.0).T, jnp.where(lane >= half, qt, 0.0).T],
                              axis=1).astype(BF16)
        keys = ks[group(t)][r * BLK:(r + 2) * BLK]
        sts[r, t] = jnp.dot(keys, qtt, preferred_element_type=F32)

    col1 = lax.broadcasted_iota(jnp.int32, (1, 2 * BLK), 1)
    feat = lax.broadcasted_iota(jnp.int32, (LANES, BLK), 0)
    kidx = lax.broadcasted_iota(jnp.int32, (2 * BLK, 2 * BLK), 0)
    col = lax.broadcasted_iota(jnp.int32, (2 * BLK, 2 * BLK), 1)
    d = kidx - (col & (BLK - 1))
    in_window = (d >= 1) & (d <= BLK)

    for r in range(SWA_QB):
        n = j * SWA_QB + r
        ok = in_window & (kidx >= PAD - (n - 1) * BLK)
        tiles = []
        for t in range(pairs):
            st = jnp.where(ok, sts[r, t], NEG)
            sink = jnp.where(col1 < BLK, sink_ref[2 * t], sink_ref[2 * t + 1]) * LOG2E
            m = jnp.maximum(jnp.max(st, axis=0, keepdims=True), sink)
            p = jnp.exp2(st - m)
            denom = jnp.sum(p, axis=0, keepdims=True) + jnp.exp2(sink - m)
            vt = vts[group(t)][:, r * BLK:(r + 2) * BLK]
            ot = jnp.dot(vt, p.astype(BF16), preferred_element_type=F32) / denom
            tiles.append(jnp.where(feat < half, ot[:, :BLK], ot[:, BLK:]).T)
        o_ref[r * BLK:(r + 1) * BLK, :] = jnp.concatenate(tiles, axis=1).astype(o_ref.dtype)


def _swa(proj, sinks, cos, sin, batch, seq):
    rows = SWA_QB * BLK
    nt = seq // rows
    kw = LANES
    prev = lambda j: jnp.maximum(j * SWA_QB - 1, 0)
    cur = lambda w, col: pl.BlockSpec((rows, w), lambda b, j: (b * nt + j, col))
    before = lambda w, col: pl.BlockSpec((BLK, w), lambda b, j: (b * nt * SWA_QB + prev(j), col))
    return pl.pallas_call(
        _swa_kernel,
        grid=(batch, nt),
        in_specs=[pl.BlockSpec(memory_space=pltpu.SMEM),
                  cur(BRANCH_W, (T_CQ * LANES) // BRANCH_W),
                  before(kw, (T_CK * LANES) // kw), cur(kw, (T_CK * LANES) // kw),
                  before(kw, (T_CV * LANES) // kw), cur(kw, (T_CV * LANES) // kw),
                  pl.BlockSpec((rows, LANES), lambda b, j: (j, 0)),
                  pl.BlockSpec((rows, LANES), lambda b, j: (j, 0)),
                  pl.BlockSpec((BLK, LANES), lambda b, j: (prev(j), 0)),
                  pl.BlockSpec((BLK, LANES), lambda b, j: (prev(j), 0))],
        out_specs=pl.BlockSpec((rows, BRANCH_W), lambda b, j: (b * nt + j, 0)),
        out_shape=jax.ShapeDtypeStruct((batch * seq, BRANCH_W), BF16),
        compiler_params=_cparams(2),
        name="swa",
    )(sinks, proj, proj, proj, proj, proj, cos, sin, cos, sin)


def _out_kernel(ya_ref, yb_ref, yc_ref, za_ref, zb_ref, zc_ref, g_ref, x_ref, head_ref, wb_ref, wo_ref, fg_ref,
                o_ref, *, first, final_norm, tiles_per_seq):
    merged = None
    for n, (y_ref, z_ref) in enumerate(((ya_ref, za_ref), (yb_ref, zb_ref), (yc_ref, zc_ref))):
        zh = z_ref[...].astype(F32)
        br = (y_ref[...].astype(F32) * (zh * (1.0 + jnp.tanh(zh)))).astype(BF16)
        pr = jnp.dot(br, wb_ref[n], preferred_element_type=F32)
        gate2 = 1.0 + jnp.tanh(g_ref[:, n * D_MODEL:(n + 1) * D_MODEL].astype(F32))
        merged = gate2 * pr if merged is None else merged + gate2 * pr
    if first:
        x = _first_layer_rows(x_ref, head_ref, pl.program_id(0) % tiles_per_seq)
    else:
        x = x_ref[...]
    h = x + jnp.dot(merged.astype(BF16), wo_ref[...], preferred_element_type=F32)
    if final_norm:
        h = h * lax.rsqrt(jnp.mean(h * h, axis=-1, keepdims=True) + EPS) * fg_ref[...]
    o_ref[...] = h


def _out(ya, yb, yc, proj, x2d, head, wb, wo, l, fg, batch, seq, first, final):
    assert not (first and final)
    m = batch * seq
    full = lambda a: pl.BlockSpec(a.shape, lambda *_: (0,) * a.ndim)
    zcol = lambda t: (t * LANES) // BRANCH_W
    if final:
        tiles = (seq - BLK) // TM_FINAL
        grid = (batch, tiles)
        row = lambda w, col: pl.BlockSpec((pl.Element(TM_FINAL), pl.Element(w)),
                                          lambda b, i: (pl.multiple_of(b * seq + BLK + i * TM_FINAL, BLK),
                                                        col * w))
        out_spec = pl.BlockSpec((TM_FINAL, D_MODEL), lambda b, i: (b * tiles + i, 0))
        out_rows = batch * (seq - BLK)
    else:
        grid = (m // TM_OUT,)
        row = lambda w, col: pl.BlockSpec((TM_OUT, w), lambda i: (i, col))
        out_spec = row(D_MODEL, 0)
        out_rows = m
    x_spec = _first_layer_spec(TM_OUT, seq, x2d.shape[0] // batch, lambda i: i) if first else row(D_MODEL, 0)
    return pl.pallas_call(
        functools.partial(_out_kernel, first=first, final_norm=final, tiles_per_seq=seq // TM_OUT),
        grid=grid,
        in_specs=[row(BRANCH_W, 0), row(BRANCH_W, 0), row(BRANCH_W, 0),
                  row(BRANCH_W, zcol(T_AZ)), row(BRANCH_W, zcol(T_BZ)), row(BRANCH_W, zcol(T_CZ)),
                  row(N_BRANCH * D_MODEL, 0), x_spec, full(head), _layer_block(wb, l), _layer_block(wo, l),
                  full(fg)],
        out_specs=out_spec,
        out_shape=jax.ShapeDtypeStruct((out_rows, D_MODEL), F32),
        compiler_params=_cparams(len(grid)),
        name="out",
    )(ya, yb, yc, proj, proj, proj, proj, x2d, head, wb, wo, fg)


def _w_in_pieces():
    sizes = (512, 512, 512, 8, 512, MLA_QLORA, MLA_KVLORA, MLA_ROPE, 512, 512, 128, 128, 512, 3072)
    offs = np.concatenate([[0], np.cumsum(sizes)])
    (a_q, a_k, a_v, a_f, a_z, b_cq, b_ckv, b_kr, b_z, c_q, c_k, c_v, c_z, gates) = [int(o) for o in offs[:-1]]
    dst = lambda t: t * LANES
    pieces = [(gates, 3072, dst(T_GATES), 0.5), (a_q, 512, dst(T_AQ), 1.0), (a_k, 512, dst(T_AK), 1.0),
              (a_v, 512, dst(T_AV), 1.0), (a_z, 512, dst(T_AZ), 0.5), (b_z, 512, dst(T_BZ), 0.5),
              (c_q, 512, dst(T_CQ), 1.0), (c_z, 512, dst(T_CZ), 0.5),
              (b_cq, MLA_QLORA, dst(T_MLA), 1.0), (b_ckv, MLA_KVLORA, dst(T_MLA) + MLA_QLORA, 1.0),
              (a_f, HEADS, dst(T_AF) + AF_LANE, 1.0),
              (c_k, SWA_KV_HEADS * HEAD_DIM, dst(T_CK), 1.0), (c_v, SWA_KV_HEADS * HEAD_DIM, dst(T_CV), 1.0)]
    assert dst(T_AF) == dst(T_MLA) + MLA_QLORA + MLA_KVLORA and AF_LANE >= 3 * MLA_ROPE
    for rep in range(3):
        pieces.append((b_kr, MLA_ROPE, dst(T_AF) + rep * MLA_ROPE, 1.0))
    return pieces, int(offs[-1])


def _relayout_kernel(w_ref, o_ref):
    o_ref[:, T_AF * LANES:(T_AF + 1) * LANES] = jnp.zeros((o_ref.shape[0], LANES), o_ref.dtype)
    for src, width, dst, scale in _w_in_pieces()[0]:
        piece = w_ref[:, src:src + width]
        if scale != 1.0:
            piece = piece * scale
        o_ref[:, dst:dst + width] = piece.astype(o_ref.dtype)


def _relayout_w_in(w):
    depth, d, n_in = w.shape
    assert n_in == _w_in_pieces()[1]
    rows = 128
    return pl.pallas_call(
        _relayout_kernel,
        grid=(depth, d // rows),
        in_specs=[pl.BlockSpec((None, rows, n_in), lambda l, r: (l, r, 0))],
        out_specs=pl.BlockSpec((None, rows, N_PROJ), lambda l, r: (l, r, 0)),
        out_shape=jax.ShapeDtypeStruct((depth, d, N_PROJ), BF16),
        compiler_params=_cparams(2),
        name="w_in_relayout",
    )(w)


def _relayout_mla_weights(w_uq, w_ukv):
    depth = w_uq.shape[0]
    uq = w_uq.reshape(depth, MLA_QLORA, HEADS, MLA_NOPE + MLA_ROPE)
    wqn = uq[..., :MLA_NOPE].reshape(depth, MLA_QLORA, HEADS * MLA_NOPE)
    rope = uq[..., MLA_NOPE:].reshape(depth, MLA_QLORA, HEADS // 2, 2 * MLA_ROPE)
    wqr = jnp.concatenate([rope, jnp.zeros_like(rope)], axis=-1).reshape(depth, MLA_QLORA, HEADS // 2 * LANES)
    ukv = w_ukv.reshape(depth, MLA_KVLORA, HEADS, MLA_NOPE + MLA_V)
    wk = ukv[..., :MLA_NOPE].reshape(depth, MLA_KVLORA, HEADS * MLA_NOPE)
    wv = ukv[..., MLA_NOPE:].reshape(depth, MLA_KVLORA, HEADS * MLA_V)
    return wqn.astype(BF16), wqr.astype(BF16), wk.astype(BF16), wv.astype(BF16)


def _rope_tables(pos, half):
    inv = ROPE_THETA ** (-jnp.arange(half, dtype=F32) / half)
    ang = pos.astype(F32)[:, None] * inv[None, :]
    cos = jnp.cos(ang)
    sin = jnp.sin(ang)
    reps = LANES // (2 * half)
    cos_full = jnp.tile(jnp.concatenate([cos, cos], axis=1), (1, reps))
    sin_signed = jnp.tile(jnp.concatenate([-sin, sin], axis=1), (1, reps))
    return cos_full, sin_signed


def kernel(x, meta_tokens, norm_g, w_in, b_f, g_cq, g_ckv, w_uq, w_ukv, sinks, w_branch, w_out, final_g):
    batch, seq_in, d = x.shape
    seq = PAD + N_META + seq_in
    depth = w_in.shape[0]
    assert d == D_MODEL and seq % TQ == 0 and seq % TM_PREP == 0
    assert seq % TM_IN == 0 and seq % TM_OUT == 0 and seq_in % TM_FINAL == 0

    head = jnp.concatenate([jnp.zeros((PAD, d), x.dtype), meta_tokens.astype(x.dtype)], axis=0)
    windowed = depth > 1
    if windowed:
        h = x.reshape(batch * seq_in, d)
    else:
        h = jnp.concatenate([jnp.broadcast_to(head[None], (batch, BLK, d)), x], axis=1).reshape(batch * seq, d)
    pos = jnp.arange(seq) - PAD
    cos16, sin16 = _rope_tables(pos, MLA_ROPE // 2)
    cos32, sin32 = _rope_tables(pos, HEAD_DIM // 2)
    fg = final_g.reshape(1, d).astype(F32)

    w = _relayout_w_in(w_in)
    wqn, wqr, wk, wv = _relayout_mla_weights(w_uq, w_ukv)
    wb = w_branch.astype(BF16)
    wo = (0.5 * w_out).astype(BF16)

    for l in range(depth):
        bf_row = jnp.zeros((1, LANES), F32).at[0, AF_LANE:AF_LANE + HEADS].set(b_f[l].astype(F32))

        first = windowed and l == 0
        proj, af = _inproj(h, head, norm_g[l].reshape(1, d).astype(F32), w, l, batch, seq, first)

        qx, kx = _fox_prep(af, bf_row, batch, seq)
        y_a = _flash(proj, T_AQ, qx, proj, T_AK, kx, proj, T_AV, batch, seq, HEAD_DIM ** -0.5, FOX_EXT, FOX_STRIDE)

        qn, qr, kn, kr, vb = _mla_prep(proj, g_cq[l].reshape(1, -1).astype(F32),
                                       g_ckv[l].reshape(1, -1).astype(F32),
                                       wqn, wqr, wk, wv, l, cos16, sin16, seq // TM_PREP)
        y_b = _flash(qn, 0, qr, kn, 0, kr, vb, 0, batch, seq, (MLA_NOPE + MLA_ROPE) ** -0.5, MLA_EXT, 0)

        y_c = _swa(proj, sinks[l].astype(F32), cos32, sin32, batch, seq)

        h = _out(y_a, y_b, y_c, proj, h, head, wb, wo, l, fg, batch, seq, first, final=(l == depth - 1))

    return h.reshape(batch, seq_in, d)
```

```python
import functools

import numpy as np
import jax
import jax.numpy as jnp
from jax import lax
from jax.experimental import pallas as pl
from jax.experimental.pallas import tpu as pltpu

F32 = jnp.float32
BF16 = jnp.bfloat16

D_MODEL = 1024
N_META = 16
BLK = 128
PAD = BLK - N_META
ROPE_THETA = 10000.0
EPS = 1e-6
NEG = -1e30

HEADS = 8
HEAD_DIM = 64
MLA_NOPE = 64
MLA_ROPE = 32
MLA_V = 64
MLA_QLORA = 384
MLA_KVLORA = 256
SWA_KV_HEADS = 2
BRANCH_W = 512
N_BRANCH = 3

LANES = 128
N_PROJ = 7680
VMEM_LIMIT = 56 * 1024 * 1024

T_GATES = 0
T_AQ = 24
T_AK = 28
T_AV = 32
T_AZ = 36
T_BZ = 40
T_CQ = 44
T_CZ = 48
T_CK = 52
T_CV = 53
T_MLA = 54
T_AF = 59
AF_LANE = 96

TM_IN = 1408
TN_IN = 2560
TM_PREP = 1408
TM_OUT = 384
TM_FINAL = 512
TQ = 384
CW = 256
NS = 4
SWA_QB = 11
LOG2E = 1.4426950408889634
FOX_EXT = 6
FOX_STRIDE = 16
MLA_EXT = MLA_ROPE


def _cparams(n_axes):
    return pltpu.CompilerParams(dimension_semantics=("arbitrary",) * n_axes,
                                vmem_limit_bytes=VMEM_LIMIT)


def _first_layer_rows(x_ref, head_ref, tile_in_seq):
    xb = x_ref[...]
    shifted = jnp.concatenate([head_ref[...], xb[:xb.shape[0] - BLK]], axis=0)
    return jnp.where(tile_in_seq == 0, shifted, xb)


def _first_layer_spec(tm, seq, seq_in, tile_of):
    tiles = seq // tm

    def index_map(*idx):
        t = tile_of(*idx)
        start = (t // tiles) * seq_in + jnp.maximum((t % tiles) * tm - BLK, 0)
        return pl.multiple_of(start, BLK), 0

    return pl.BlockSpec((pl.Element(tm), pl.Element(D_MODEL)), index_map)


def _inproj_kernel(x_ref, head_ref, g_ref, w_ref, p_ref, af_ref, h_scr, *, first, tiles_per_seq):
    j = pl.program_id(1)

    @pl.when(j == 0)
    def _():
        if first:
            x = _first_layer_rows(x_ref, head_ref, pl.program_id(0) % tiles_per_seq)
        else:
            x = x_ref[...]
        ms = jnp.mean(x * x, axis=-1, keepdims=True)
        h_scr[...] = (x * lax.rsqrt(ms + EPS) * g_ref[...]).astype(BF16)

    acc = jnp.dot(h_scr[...], w_ref[...], preferred_element_type=F32)
    p_ref[...] = acc.astype(BF16)

    @pl.when(j == (T_AF * LANES) // TN_IN)
    def _():
        off = (T_AF * LANES) % TN_IN
        af_ref[...] = acc[:, off:off + LANES]


def _layer_block(a, l):
    return pl.BlockSpec((None,) + a.shape[1:], lambda *_: (l,) + (0,) * (a.ndim - 1))


def _inproj(x2d, head, g, w, l, batch, seq, first):
    m = batch * seq
    if first:
        x_spec = _first_layer_spec(TM_IN, seq, x2d.shape[0] // batch, lambda i, j: i)
    else:
        x_spec = pl.BlockSpec((TM_IN, D_MODEL), lambda i, j: (i, 0))
    return pl.pallas_call(
        functools.partial(_inproj_kernel, first=first, tiles_per_seq=seq // TM_IN),
        grid=(m // TM_IN, N_PROJ // TN_IN),
        in_specs=[x_spec,
                  pl.BlockSpec(head.shape, lambda i, j: (0, 0)),
                  pl.BlockSpec((1, D_MODEL), lambda i, j: (0, 0)),
                  pl.BlockSpec((None, D_MODEL, TN_IN), lambda i, j: (l, 0, j))],
        out_specs=[pl.BlockSpec((TM_IN, TN_IN), lambda i, j: (i, j)),
                   pl.BlockSpec((TM_IN, LANES), lambda i, j: (i, 0))],
        out_shape=[jax.ShapeDtypeStruct((m, N_PROJ), BF16),
                   jax.ShapeDtypeStruct((m, LANES), F32)],
        scratch_shapes=[pltpu.VMEM((TM_IN, D_MODEL), BF16)],
        compiler_params=_cparams(2),
        name="inproj",
    )(x2d, head, g, w)


def _rope(x, cos, sin_signed, half):
    width = x.shape[1]
    reps = width // LANES
    if reps > 1:
        cos = jnp.concatenate([cos] * reps, axis=1)
        sin_signed = jnp.concatenate([sin_signed] * reps, axis=1)
    lane = lax.broadcasted_iota(jnp.int32, x.shape, 1)
    up = pltpu.roll(x, width - half, axis=1)
    down = pltpu.roll(x, half, axis=1)
    swapped = jnp.where((lane & (2 * half - 1)) < half, up, down)
    return x * cos + swapped * sin_signed


def _mla_prep_kernel(p_ref, gq_ref, gkv_ref, wqn_ref, wqr_ref, wk_ref, wv_ref, cos_ref, sin_ref,
                     qn_ref, qr_ref, kn_ref, kr_ref, v_ref, *, seq_tiles):
    blk = p_ref[...].astype(F32)
    cq = blk[:, :MLA_QLORA]
    ckv = blk[:, MLA_QLORA:MLA_QLORA + MLA_KVLORA]
    kr = blk[:, MLA_QLORA + MLA_KVLORA:]
    cq = (cq * lax.rsqrt(jnp.mean(cq * cq, axis=-1, keepdims=True) + EPS) * gq_ref[...]).astype(BF16)
    ckv = (ckv * lax.rsqrt(jnp.mean(ckv * ckv, axis=-1, keepdims=True) + EPS) * gkv_ref[...]).astype(BF16)
    scale = (MLA_NOPE + MLA_ROPE) ** -0.5
    cos = cos_ref[...]
    sin = sin_ref[...]
    qn_ref[...] = jnp.dot(cq, wqn_ref[...], preferred_element_type=F32).astype(BF16)
    qr = jnp.dot(cq, wqr_ref[...], preferred_element_type=F32)
    qr = _rope(qr, cos, sin, MLA_ROPE // 2) * (scale * LOG2E)
    qlane = lax.broadcasted_iota(jnp.int32, qr.shape, 1) & (LANES - 1)
    qr_ref[...] = jnp.where(qlane == 2 * MLA_EXT, 1.0, qr).astype(BF16)
    kn_ref[...] = jnp.dot(ckv, wk_ref[...], preferred_element_type=F32).astype(BF16)
    v_ref[...] = jnp.dot(ckv, wv_ref[...], preferred_element_type=F32).astype(BF16)
    kr = _rope(kr, cos, sin, MLA_ROPE // 2)
    pos_in_seq = (pl.program_id(0) % seq_tiles) * kr.shape[0] + lax.broadcasted_iota(jnp.int32, kr.shape, 0)
    klane = lax.broadcasted_iota(jnp.int32, kr.shape, 1)
    kr_ref[...] = jnp.where(klane == 2 * MLA_EXT, jnp.where(pos_in_seq < PAD, NEG, 0.0), kr).astype(BF16)


def _mla_prep(proj, gq, gkv, wqn, wqr, wk, wv, l, cos, sin, seq_tiles):
    m = proj.shape[0]
    full = lambda a: pl.BlockSpec(a.shape, lambda i: (0,) * a.ndim)
    layer = lambda a: _layer_block(a, l)
    row = lambda w: pl.BlockSpec((TM_PREP, w), lambda i: (i, 0))
    tab = pl.BlockSpec((TM_PREP, LANES), lambda i: (i % seq_tiles, 0))
    mla_w = MLA_QLORA + MLA_KVLORA + LANES
    return pl.pallas_call(
        functools.partial(_mla_prep_kernel, seq_tiles=seq_tiles),
        grid=(m // TM_PREP,),
        in_specs=[pl.BlockSpec((TM_PREP, mla_w), lambda i: (i, (T_MLA * LANES) // mla_w)),
                  full(gq), full(gkv), layer(wqn), layer(wqr), layer(wk), layer(wv), tab, tab],
        out_specs=[row(BRANCH_W), row(BRANCH_W), row(BRANCH_W), row(LANES), row(BRANCH_W)],
        out_shape=[jax.ShapeDtypeStruct((m, BRANCH_W), BF16),
                   jax.ShapeDtypeStruct((m, BRANCH_W), BF16),
                   jax.ShapeDtypeStruct((m, BRANCH_W), BF16),
                   jax.ShapeDtypeStruct((m, LANES), BF16),
                   jax.ShapeDtypeStruct((m, BRANCH_W), BF16)],
        compiler_params=_cparams(1),
        name="mla_prep",
    )(proj, gq, gkv, wqn, wqr, wk, wv, cos, sin)


def _split3(x):
    hi = x.astype(BF16)
    r1 = x - hi.astype(F32)
    mid = r1.astype(BF16)
    lo = (r1 - mid.astype(F32)).astype(BF16)
    return jnp.concatenate([hi, mid, lo], axis=1)


def _fox_prep_kernel(af_ref, bf_ref, selq_ref, selk_ref, oneq_ref, onek_ref, qx_ref, kx_ref,
                     lf_scr, c_scr):
    seq = af_ref.shape[0]
    x = af_ref[...] + bf_ref[...]
    lf_scr[...] = -(jnp.maximum(-x, 0.0) + jnp.log1p(jnp.exp(-jnp.abs(x))))
    r = lax.broadcasted_iota(jnp.int32, (BLK, BLK), 0)
    c = lax.broadcasted_iota(jnp.int32, (BLK, BLK), 1)
    tri = (c <= r).astype(BF16)

    for t in range(seq // BLK):
        rows = slice(t * BLK, (t + 1) * BLK)
        cs = jnp.dot(tri, _split3(lf_scr[rows, :]), preferred_element_type=F32)
        c_scr[rows, :] = cs[:, :LANES] + cs[:, LANES:2 * LANES] + cs[:, 2 * LANES:]
    totals = [c_scr[t * BLK - 1:t * BLK, :] for t in range(1, seq // BLK)]
    carry = jnp.zeros((1, LANES), F32)
    for t in range(1, seq // BLK):
        carry = carry + totals[t - 1]
        c_scr[t * BLK:(t + 1) * BLK, :] = c_scr[t * BLK:(t + 1) * BLK, :] + carry
    parts = _split3(c_scr[...] * LOG2E)
    qx_ref[...] = (jnp.dot(parts, selq_ref[...], preferred_element_type=F32) + oneq_ref[...]).astype(BF16)
    kx_ref[...] = (jnp.dot(parts, selk_ref[...], preferred_element_type=F32) + onek_ref[...]).astype(BF16)
    row = lax.broadcasted_iota(jnp.int32, (BLK, kx_ref.shape[1]), 0)
    lane = lax.broadcasted_iota(jnp.int32, (BLK, kx_ref.shape[1]), 1)
    pad_bias = (row < PAD) & ((lane & (FOX_STRIDE - 1)) == 2 * FOX_EXT) & (lane < HEADS // 2 * FOX_STRIDE)
    kx_ref[:BLK, :] = jnp.where(pad_bias, NEG, kx_ref[:BLK, :].astype(F32)).astype(BF16)


def _fox_select_constants():
    selq = np.zeros((3 * LANES, LANES), np.float32)
    selk = np.zeros((3 * LANES, LANES), np.float32)
    oneq = np.zeros((1, LANES), np.float32)
    onek = np.zeros((1, LANES), np.float32)
    for p in range(HEADS // 2):
        for s in range(2):
            h = 2 * p + s
            base = p * FOX_STRIDE + s * FOX_EXT
            for part in range(3):
                selq[part * LANES + AF_LANE + h, base + part] = 1.0
                selk[part * LANES + AF_LANE + h, base + 3 + part] = -1.0
                oneq[0, base + 3 + part] = 1.0
                onek[0, base + part] = 1.0
        oneq[0, p * FOX_STRIDE + 2 * FOX_EXT] = 1.0
    return (jnp.asarray(selq, BF16), jnp.asarray(selk, BF16), jnp.asarray(oneq), jnp.asarray(onek))


def _fox_prep(af, bf_row, batch, seq):
    selq, selk, oneq, onek = _fox_select_constants()
    full = lambda a: pl.BlockSpec(a.shape, lambda b: (0,) * a.ndim)
    return pl.pallas_call(
        _fox_prep_kernel,
        grid=(batch,),
        in_specs=[pl.BlockSpec((seq, LANES), lambda b: (b, 0)),
                  full(bf_row), full(selq), full(selk), full(oneq), full(onek)],
        out_specs=[pl.BlockSpec((seq, LANES), lambda b: (b, 0)),
                   pl.BlockSpec((seq, LANES), lambda b: (b, 0))],
        out_shape=[jax.ShapeDtypeStruct((batch * seq, LANES), BF16),
                   jax.ShapeDtypeStruct((batch * seq, LANES), BF16)],
        scratch_shapes=[pltpu.VMEM((seq, LANES), F32), pltpu.VMEM((seq, LANES), F32)],
        compiler_params=_cparams(1),
        name="fox_prep",
    )(af, bf_row, selq, selk, oneq, onek)


def _flash_kernel(qm_ref, qx_ref, km_ref, kx_ref, v_ref, o_ref, vt_scr, qt_scr, m_scr, l_scr, acc_scr,
                  sa_scr, sb_scr, *, scale, ext, x_stride):
    i = pl.program_id(2)
    half = LANES // 2
    nq2 = 2 * TQ
    tile = lambda s: slice(s * LANES, (s + 1) * LANES)
    qx_tile = tile if x_stride == 0 else (lambda s: slice(0, LANES))

    @pl.when(i == 0)
    def _():
        for s in range(NS):
            for c in range(v_ref.shape[0] // TQ):
                vt_scr[s, c] = v_ref[c * TQ:(c + 1) * TQ, tile(s)].astype(F32).T.astype(BF16)

    lane = lax.broadcasted_iota(jnp.int32, (TQ, LANES), 1)
    for s in range(NS):
        qm = qm_ref[:, tile(s)].astype(F32) * (scale * LOG2E)
        qx = qx_ref[:, qx_tile(s)].astype(F32)
        x0 = s * x_stride
        shared = lane == x0 + 2 * ext
        qa = jnp.concatenate([jnp.where(lane < half, qm, 0.0),
                              jnp.where(((lane >= x0) & (lane < x0 + ext)) | shared, qx, 0.0)], axis=1)
        qb = jnp.concatenate([jnp.where(lane >= half, qm, 0.0),
                              jnp.where(((lane >= x0 + ext) & (lane < x0 + 2 * ext)) | shared, qx, 0.0)], axis=1)
        for u in range(TQ // BLK):
            rows = slice(u * BLK, (u + 1) * BLK)
            qt_scr[s, :, u * CW:u * CW + BLK] = qa[rows].T.astype(BF16)
            qt_scr[s, :, u * CW + BLK:(u + 1) * CW] = qb[rows].T.astype(BF16)

    m_scr[...] = jnp.full(m_scr.shape, NEG, F32)
    l_scr[...] = jnp.zeros(l_scr.shape, F32)
    acc_scr[...] = jnp.zeros(acc_scr.shape, F32)

    def visible_keys(u, diagonal):
        return (u + 1) * BLK if diagonal else TQ

    def scores(c, s_ref, diagonal=False):
        off = pl.multiple_of(c * TQ, TQ)
        for s in range(NS):
            k = jnp.concatenate([km_ref[pl.ds(off, TQ), tile(s)], kx_ref[pl.ds(off, TQ), :]], axis=1)
            for u in range(TQ // BLK):
                nk = visible_keys(u, diagonal)
                s_ref[s, :nk, u * CW:(u + 1) * CW] = jnp.dot(k[:nk], qt_scr[s, :, u * CW:(u + 1) * CW],
                                                             preferred_element_type=F32)

    def update(c, s_ref, diagonal):
        for s in range(NS):
            vt = vt_scr[s, c]
            for u in range(TQ // BLK):
                cols = slice(u * CW, (u + 1) * CW)
                nk = visible_keys(u, diagonal)
                st = s_ref[s, :nk, cols]
                if diagonal:
                    key = lax.broadcasted_iota(jnp.int32, (nk, CW), 0)
                    query = u * BLK + (lax.broadcasted_iota(jnp.int32, (nk, CW), 1) & (BLK - 1))
                    st = jnp.where(key <= query, st, NEG)
                m_old = m_scr[s, :, cols]
                m_new = jnp.maximum(m_old, jnp.max(st, axis=0, keepdims=True))
                alpha = jnp.exp2(m_old - m_new)
                pt = jnp.exp2(st - m_new)
                l_scr[s, :, cols] = alpha * l_scr[s, :, cols] + jnp.sum(pt, axis=0, keepdims=True)
                pv = jnp.dot(vt[:, :nk], pt.astype(BF16), preferred_element_type=F32)
                acc_scr[s, :, cols] = alpha * acc_scr[s, :, cols] + pv
                m_scr[s, :, cols] = m_new

    scores(0, sa_scr)

    def pair(j, carry):
        c = 2 * j
        scores(c + 1, sb_scr)
        update(c, sa_scr, False)
        scores(c + 2, sa_scr)
        update(c + 1, sb_scr, False)
        return carry

    npairs = i // 2
    lax.fori_loop(0, npairs, pair, 0)
    odd_tail = i - 2 * npairs == 1

    @pl.when(odd_tail)
    def _():
        scores(i, sb_scr, diagonal=True)
        update(i - 1, sa_scr, False)
        update(i, sb_scr, True)

    @pl.when(jnp.logical_not(odd_tail))
    def _():
        update(i, sa_scr, True)

    feat = lax.broadcasted_iota(jnp.int32, (LANES, BLK), 0)
    for s in range(NS):
        out = acc_scr[s] / l_scr[s]
        for u in range(TQ // BLK):
            both = out[:, u * CW:(u + 1) * CW]
            o_ref[u * BLK:(u + 1) * BLK, tile(s)] = jnp.where(feat < half, both[:, :BLK],
                                                              both[:, BLK:]).T.astype(o_ref.dtype)


def _flash(qm, qm_tile, qx, km, km_tile, kx, v, v_tile, batch, seq, scale, ext, x_stride):
    nq = seq // TQ
    groups = HEADS // 2 // NS
    w = NS * LANES
    assert qm_tile % NS == 0 and km_tile % NS == 0 and v_tile % NS == 0
    if x_stride:
        assert groups == 1
        qx_spec = pl.BlockSpec((TQ, LANES), lambda b, p, i: (b * nq + i, 0))
    else:
        qx_spec = pl.BlockSpec((TQ, w), lambda b, p, i: (b * nq + i, p))
    return pl.pallas_call(
        functools.partial(_flash_kernel, scale=scale, ext=ext, x_stride=x_stride),
        grid=(batch, groups, nq),
        in_specs=[pl.BlockSpec((TQ, w), lambda b, p, i: (b * nq + i, qm_tile // NS + p)),
                  qx_spec,
                  pl.BlockSpec((seq, w), lambda b, p, i: (b, km_tile // NS + p)),
                  pl.BlockSpec((seq, LANES), lambda b, p, i: (b, 0)),
                  pl.BlockSpec((seq, w), lambda b, p, i: (b, v_tile // NS + p))],
        out_specs=pl.BlockSpec((TQ, w), lambda b, p, i: (b * nq + i, p)),
        out_shape=jax.ShapeDtypeStruct((batch * seq, BRANCH_W), BF16),
        scratch_shapes=[pltpu.VMEM((NS, seq // TQ, LANES, TQ), BF16), pltpu.VMEM((NS, 2 * LANES, 2 * TQ), BF16),
                        pltpu.VMEM((NS, 1, 2 * TQ), F32), pltpu.VMEM((NS, 1, 2 * TQ), F32),
                        pltpu.VMEM((NS, LANES, 2 * TQ), F32),
                        pltpu.VMEM((NS, TQ, 2 * TQ), F32), pltpu.VMEM((NS, TQ, 2 * TQ), F32)],
        compiler_params=_cparams(3),
        name="flash",
    )(qm, qx, km, kx, v)


def _swa_kernel(sink_ref, q_ref, kp_ref, kc_ref, vp_ref, vc_ref, cosc_ref, sinc_ref, cosp_ref, sinp_ref,
                o_ref):
    j = pl.program_id(1)
    half = LANES // 2
    rh = HEAD_DIM // 2
    lane = lax.broadcasted_iota(jnp.int32, (BLK, LANES), 1)
    q = _rope(q_ref[...].astype(F32), cosc_ref[...], sinc_ref[...], rh) * (HEAD_DIM ** -0.5 * LOG2E)
    kc = _rope(kc_ref[...].astype(F32), cosc_ref[...], sinc_ref[...], rh)
    kp = _rope(kp_ref[...].astype(F32), cosp_ref[...], sinp_ref[...], rh)
    pairs = HEADS // 2
    group = lambda t: t // (pairs // SWA_KV_HEADS)

    def per_group(x):
        lane_k = lax.broadcasted_iota(jnp.int32, x.shape, 1)
        swapped = pltpu.roll(x, half, axis=1)
        return [jnp.where(lane_k < half, x, swapped), jnp.where(lane_k < half, swapped, x)]

    ks = [g.astype(BF16) for g in per_group(jnp.concatenate([kp, kc], axis=0))]
    v = jnp.concatenate([vp_ref[...], vc_ref[...]], axis=0).astype(F32)
    vts = [g.T.astype(BF16) for g in per_group(v)]

    units = [(r, t) for r in range(SWA_QB) for t in range(pairs)]
    sts = {}
    for r, t in units:
        qt = q[r * BLK:(r + 1) * BLK, t * LANES:(t + 1) * LANES]
        qtt = jnp.concatenate([jnp.where(lane < half, qt, 0.0).T, jnp.where(lane >= half, qt, 0.0).T],
                              axis=1).astype(BF16)
        keys = ks[group(t)][r * BLK:(r + 2) * BLK]
        sts[r, t] = jnp.dot(keys, qtt, preferred_element_type=F32)

    col1 = lax.broadcasted_iota(jnp.int32, (1, 2 * BLK), 1)
    feat = lax.broadcasted_iota(jnp.int32, (LANES, BLK), 0)
    kidx = lax.broadcasted_iota(jnp.int32, (2 * BLK, 2 * BLK), 0)
    col = lax.broadcasted_iota(jnp.int32, (2 * BLK, 2 * BLK), 1)
    d = kidx - (col & (BLK - 1))
    in_window = (d >= 1) & (d <= BLK)

    for r in range(SWA_QB):
        n = j * SWA_QB + r
        ok = in_window & (kidx >= PAD - (n - 1) * BLK)
        tiles = []
        for t in range(pairs):
            st = jnp.where(ok, sts[r, t], NEG)
            sink = jnp.where(col1 < BLK, sink_ref[2 * t], sink_ref[2 * t + 1]) * LOG2E
            m = jnp.maximum(jnp.max(st, axis=0, keepdims=True), sink)
            p = jnp.exp2(st - m)
            denom = jnp.sum(p, axis=0, keepdims=True) + jnp.exp2(sink - m)
            vt = vts[group(t)][:, r * BLK:(r + 2) * BLK]
            ot = jnp.dot(vt, p.astype(BF16), preferred_element_type=F32) / denom
            tiles.append(jnp.where(feat < half, ot[:, :BLK], ot[:, BLK:]).T)
        o_ref[r * BLK:(r + 1) * BLK, :] = jnp.concatenate(tiles, axis=1).astype(o_ref.dtype)


def _swa(proj, sinks, cos, sin, batch, seq):
    rows = SWA_QB * BLK
    nt = seq // rows
    kw = LANES
    prev = lambda j: jnp.maximum(j * SWA_QB - 1, 0)
    cur = lambda w, col: pl.BlockSpec((rows, w), lambda b, j: (b * nt + j, col))
    before = lambda w, col: pl.BlockSpec((BLK, w), lambda b, j: (b * nt * SWA_QB + prev(j), col))
    return pl.pallas_call(
        _swa_kernel,
        grid=(batch, nt),
        in_specs=[pl.BlockSpec(memory_space=pltpu.SMEM),
                  cur(BRANCH_W, (T_CQ * LANES) // BRANCH_W),
                  before(kw, (T_CK * LANES) // kw), cur(kw, (T_CK * LANES) // kw),
                  before(kw, (T_CV * LANES) // kw), cur(kw, (T_CV * LANES) // kw),
                  pl.BlockSpec((rows, LANES), lambda b, j: (j, 0)),
                  pl.BlockSpec((rows, LANES), lambda b, j: (j, 0)),
                  pl.BlockSpec((BLK, LANES), lambda b, j: (prev(j), 0)),
                  pl.BlockSpec((BLK, LANES), lambda b, j: (prev(j), 0))],
        out_specs=pl.BlockSpec((rows, BRANCH_W), lambda b, j: (b * nt + j, 0)),
        out_shape=jax.ShapeDtypeStruct((batch * seq, BRANCH_W), BF16),
        compiler_params=_cparams(2),
        name="swa",
    )(sinks, proj, proj, proj, proj, proj, cos, sin, cos, sin)


def _out_kernel(ya_ref, yb_ref, yc_ref, za_ref, zb_ref, zc_ref, g_ref, x_ref, head_ref, wb_ref, wo_ref, fg_ref,
                o_ref, *, first, final_norm, tiles_per_seq):
    merged = None
    for n, (y_ref, z_ref) in enumerate(((ya_ref, za_ref), (yb_ref, zb_ref), (yc_ref, zc_ref))):
        zh = z_ref[...].astype(F32)
        br = (y_ref[...].astype(F32) * (zh * (1.0 + jnp.tanh(zh)))).astype(BF16)
        pr = jnp.dot(br, wb_ref[n], preferred_element_type=F32)
        gate2 = 1.0 + jnp.tanh(g_ref[:, n * D_MODEL:(n + 1) * D_MODEL].astype(F32))
        merged = gate2 * pr if merged is None else merged + gate2 * pr
    if first:
        x = _first_layer_rows(x_ref, head_ref, pl.program_id(0) % tiles_per_seq)
    else:
        x = x_ref[...]
    h = x + jnp.dot(merged.astype(BF16), wo_ref[...], preferred_element_type=F32)
    if final_norm:
        h = h * lax.rsqrt(jnp.mean(h * h, axis=-1, keepdims=True) + EPS) * fg_ref[...]
    o_ref[...] = h


def _out(ya, yb, yc, proj, x2d, head, wb, wo, l, fg, batch, seq, first, final):
    assert not (first and final)
    m = batch * seq
    full = lambda a: pl.BlockSpec(a.shape, lambda *_: (0,) * a.ndim)
    zcol = lambda t: (t * LANES) // BRANCH_W
    if final:
        tiles = (seq - BLK) // TM_FINAL
        grid = (batch, tiles)
        row = lambda w, col: pl.BlockSpec((pl.Element(TM_FINAL), pl.Element(w)),
                                          lambda b, i: (pl.multiple_of(b * seq + BLK + i * TM_FINAL, BLK),
                                                        col * w))
        out_spec = pl.BlockSpec((TM_FINAL, D_MODEL), lambda b, i: (b * tiles + i, 0))
        out_rows = batch * (seq - BLK)
    else:
        grid = (m // TM_OUT,)
        row = lambda w, col: pl.BlockSpec((TM_OUT, w), lambda i: (i, col))
        out_spec = row(D_MODEL, 0)
        out_rows = m
    x_spec = _first_layer_spec(TM_OUT, seq, x2d.shape[0] // batch, lambda i: i) if first else row(D_MODEL, 0)
    return pl.pallas_call(
        functools.partial(_out_kernel, first=first, final_norm=final, tiles_per_seq=seq // TM_OUT),
        grid=grid,
        in_specs=[row(BRANCH_W, 0), row(BRANCH_W, 0), row(BRANCH_W, 0),
                  row(BRANCH_W, zcol(T_AZ)), row(BRANCH_W, zcol(T_BZ)), row(BRANCH_W, zcol(T_CZ)),
                  row(N_BRANCH * D_MODEL, 0), x_spec, full(head), _layer_block(wb, l), _layer_block(wo, l),
                  full(fg)],
        out_specs=out_spec,
        out_shape=jax.ShapeDtypeStruct((out_rows, D_MODEL), F32),
        compiler_params=_cparams(len(grid)),
        name="out",
    )(ya, yb, yc, proj, proj, proj, proj, x2d, head, wb, wo, fg)


def _w_in_pieces():
    sizes = (512, 512, 512, 8, 512, MLA_QLORA, MLA_KVLORA, MLA_ROPE, 512, 512, 128, 128, 512, 3072)
    offs = np.concatenate([[0], np.cumsum(sizes)])
    (a_q, a_k, a_v, a_f, a_z, b_cq, b_ckv, b_kr, b_z, c_q, c_k, c_v, c_z, gates) = [int(o) for o in offs[:-1]]
    dst = lambda t: t * LANES
    pieces = [(gates, 3072, dst(T_GATES), 0.5), (a_q, 512, dst(T_AQ), 1.0), (a_k, 512, dst(T_AK), 1.0),
              (a_v, 512, dst(T_AV), 1.0), (a_z, 512, dst(T_AZ), 0.5), (b_z, 512, dst(T_BZ), 0.5),
              (c_q, 512, dst(T_CQ), 1.0), (c_z, 512, dst(T_CZ), 0.5),
              (b_cq, MLA_QLORA, dst(T_MLA), 1.0), (b_ckv, MLA_KVLORA, dst(T_MLA) + MLA_QLORA, 1.0),
              (a_f, HEADS, dst(T_AF) + AF_LANE, 1.0),
              (c_k, SWA_KV_HEADS * HEAD_DIM, dst(T_CK), 1.0), (c_v, SWA_KV_HEADS * HEAD_DIM, dst(T_CV), 1.0)]
    assert dst(T_AF) == dst(T_MLA) + MLA_QLORA + MLA_KVLORA and AF_LANE >= 3 * MLA_ROPE
    for rep in range(3):
        pieces.append((b_kr, MLA_ROPE, dst(T_AF) + rep * MLA_ROPE, 1.0))
    return pieces, int(offs[-1])


def _relayout_kernel(w_ref, o_ref):
    o_ref[:, T_AF * LANES:(T_AF + 1) * LANES] = jnp.zeros((o_ref.shape[0], LANES), o_ref.dtype)
    for src, width, dst, scale in _w_in_pieces()[0]:
        piece = w_ref[:, src:src + width]
        if scale != 1.0:
            piece = piece * scale
        o_ref[:, dst:dst + width] = piece.astype(o_ref.dtype)


def _relayout_w_in(w):
    depth, d, n_in = w.shape
    assert n_in == _w_in_pieces()[1]
    rows = 128
    return pl.pallas_call(
        _relayout_kernel,
        grid=(depth, d // rows),
        in_specs=[pl.BlockSpec((None, rows, n_in), lambda l, r: (l, r, 0))],
        out_specs=pl.BlockSpec((None, rows, N_PROJ), lambda l, r: (l, r, 0)),
        out_shape=jax.ShapeDtypeStruct((depth, d, N_PROJ), BF16),
        compiler_params=_cparams(2),
        name="w_in_relayout",
    )(w)


def _relayout_mla_weights(w_uq, w_ukv):
    depth = w_uq.shape[0]
    uq = w_uq.reshape(depth, MLA_QLORA, HEADS, MLA_NOPE + MLA_ROPE)
    wqn = uq[..., :MLA_NOPE].reshape(depth, MLA_QLORA, HEADS * MLA_NOPE)
    rope = uq[..., MLA_NOPE:].reshape(depth, MLA_QLORA, HEADS // 2, 2 * MLA_ROPE)
    wqr = jnp.concatenate([rope, jnp.zeros_like(rope)], axis=-1).reshape(depth, MLA_QLORA, HEADS // 2 * LANES)
    ukv = w_ukv.reshape(depth, MLA_KVLORA, HEADS, MLA_NOPE + MLA_V)
    wk = ukv[..., :MLA_NOPE].reshape(depth, MLA_KVLORA, HEADS * MLA_NOPE)
    wv = ukv[..., MLA_NOPE:].reshape(depth, MLA_KVLORA, HEADS * MLA_V)
    return wqn.astype(BF16), wqr.astype(BF16), wk.astype(BF16), wv.astype(BF16)


def _rope_tables(pos, half):
    inv = ROPE_THETA ** (-jnp.arange(half, dtype=F32) / half)
    ang = pos.astype(F32)[:, None] * inv[None, :]
    cos = jnp.cos(ang)
    sin = jnp.sin(ang)
    reps = LANES // (2 * half)
    cos_full = jnp.tile(jnp.concatenate([cos, cos], axis=1), (1, reps))
    sin_signed = jnp.tile(jnp.concatenate([-sin, sin], axis=1), (1, reps))
    return cos_full, sin_signed


def kernel(x, meta_tokens, norm_g, w_in, b_f, g_cq, g_ckv, w_uq, w_ukv, sinks, w_branch, w_out, final_g):
    batch, seq_in, d = x.shape
    seq = PAD + N_META + seq_in
    depth = w_in.shape[0]
    assert d == D_MODEL and seq % TQ == 0 and seq % TM_PREP == 0
    assert seq % TM_IN == 0 and seq % TM_OUT == 0 and seq_in % TM_FINAL == 0

    head = jnp.concatenate([jnp.zeros((PAD, d), x.dtype), meta_tokens.astype(x.dtype)], axis=0)
    windowed = depth > 1
    if windowed:
        h = x.reshape(batch * seq_in, d)
    else:
        h = jnp.concatenate([jnp.broadcast_to(head[None], (batch, BLK, d)), x], axis=1).reshape(batch * seq, d)
    pos = jnp.arange(seq) - PAD
    cos16, sin16 = _rope_tables(pos, MLA_ROPE // 2)
    cos32, sin32 = _rope_tables(pos, HEAD_DIM // 2)
    fg = final_g.reshape(1, d).astype(F32)

    w = _relayout_w_in(w_in)
    wqn, wqr, wk, wv = _relayout_mla_weights(w_uq, w_ukv)
    wb = w_branch.astype(BF16)
    wo = (0.5 * w_out).astype(BF16)

    for l in range(depth):
        bf_row = jnp.zeros((1, LANES), F32).at[0, AF_LANE:AF_LANE + HEADS].set(b_f[l].astype(F32))

        first = windowed and l == 0
        proj, af = _inproj(h, head, norm_g[l].reshape(1, d).astype(F32), w, l, batch, seq, first)

        qx, kx = _fox_prep(af, bf_row, batch, seq)
        y_a = _flash(proj, T_AQ, qx, proj, T_AK, kx, proj, T_AV, batch, seq, HEAD_DIM ** -0.5, FOX_EXT, FOX_STRIDE)

        qn, qr, kn, kr, vb = _mla_prep(proj, g_cq[l].reshape(1, -1).astype(F32),
                                       g_ckv[l].reshape(1, -1).astype(F32),
                                       wqn, wqr, wk, wv, l, cos16, sin16, seq // TM_PREP)
        y_b = _flash(qn, 0, qr, kn, 0, kr, vb, 0, batch, seq, (MLA_NOPE + MLA_ROPE) ** -0.5, MLA_EXT, 0)

        y_c = _swa(proj, sinks[l].astype(F32), cos32, sin32, batch, seq)

        h = _out(y_a, y_b, y_c, proj, h, head, wb, wo, l, fg, batch, seq, first, final=(l == depth - 1))

    return h.reshape(batch, seq_in, d)
```
